```python
import jax, jax.numpy as jnp
from jax import lax
import numpy as np

D_MODEL = 1024
BATCH = 8
SEQ = 2048
DEPTH = 1
DEC_BATCH = 128
DEC_SEQ = 4
PAST_LEN = 16384
PAGE_SIZE = 128

N_META = 16
BLOCK = 128
EPS = 1e-6
SB_HEADS = 8
SB_KV_HEADS = 4
SB_HEAD_DIM = 64
SB_REP = SB_HEADS // SB_KV_HEADS
SB_WIDTH = SB_HEADS * SB_HEAD_DIM
SB_SCALE = SB_HEAD_DIM ** -0.5
MLA_HEADS = 8
MLA_Q_LORA = 256
MLA_KV_LORA = 128
MLA_NOPE_DIM = 64
MLA_ROPE_DIM = 32
MLA_V_DIM = 64
MLA_WIDTH = MLA_HEADS * MLA_V_DIM
MLA_SCALE = (MLA_NOPE_DIM + MLA_ROPE_DIM) ** -0.5
ROPE_THETA = 10000.0
MIX_WIDTH = SB_WIDTH + MLA_WIDTH
C_SB_Q = SB_HEADS * SB_HEAD_DIM
C_SB_K = C_SB_Q + SB_KV_HEADS * SB_HEAD_DIM
C_SB_V = C_SB_K + SB_KV_HEADS * SB_HEAD_DIM
C_CQ = C_SB_V + MLA_Q_LORA
C_CKV = C_CQ + MLA_KV_LORA
IN_COLS = C_CKV + MLA_ROPE_DIM
N_GROUPS = 4
EXPERTS_PER_GROUP = 8
N_EXPERTS = N_GROUPS * EXPERTS_PER_GROUP
TOP_K_INNER = 2
D_EXPERT = 256

kernel_name = 'hymba_stickbreak_mla_hmoe_step'


def rmsnorm(x, g):
    xf = x.astype(jnp.float32)
    y = xf * lax.rsqrt(jnp.mean(xf * xf, axis=-1, keepdims=True) + EPS)
    return (y * g.astype(jnp.float32)).astype(x.dtype)


def rope(x, pos):
    half = MLA_ROPE_DIM // 2
    inv_freq = ROPE_THETA ** (-jnp.arange(half, dtype=jnp.float32) / half)
    ang = pos.astype(jnp.float32)[:, None] * inv_freq[None, :]
    shape = (ang.shape[0],) + (1,) * (x.ndim - 3) + (half,)
    cos = jnp.cos(ang).reshape(shape)
    sin = jnp.sin(ang).reshape(shape)
    xf = x.astype(jnp.float32)
    x1, x2 = xf[..., :half], xf[..., half:]
    return jnp.concatenate([x1 * cos - x2 * sin, x2 * cos + x1 * sin], axis=-1).astype(x.dtype)


def query_blocks(length):
    return [(0, N_META)] + [(s, min(s + BLOCK, length)) for s in range(N_META, length, BLOCK)]


def sb_weights(z, visible):
    log_1mb = jnp.where(visible, jax.nn.log_sigmoid(-z), 0.0)
    later = lax.cumsum(log_1mb, axis=z.ndim - 1, reverse=True) - log_1mb
    return jnp.where(visible, jnp.exp(jax.nn.log_sigmoid(z) + later), 0.0)


def key_inv_rms(k_raw):
    kf = k_raw.astype(jnp.float32)
    return lax.rsqrt(jnp.mean(kf * kf, axis=-1) + EPS)


def mla_scores(q_nope_g, q_rope, k_raw, k_inv, k_rope):
    s_n = jnp.einsum('bqhd,bkhd->bhqk', q_nope_g, k_raw).astype(jnp.float32)
    s_n = s_n * jnp.swapaxes(k_inv, 1, 2)[:, :, None, :]
    s_r = jnp.einsum('bqhd,bkd->bhqk', q_rope, k_rope).astype(jnp.float32)
    return (s_n + s_r) * MLA_SCALE


def mixer_inputs(x, pos, norm1_g, w_in, cq_g, ckv_g, w_uq, qn_g, qr_g, kr_g):
    b, length, _ = x.shape
    p = jnp.dot(rmsnorm(x, norm1_g), w_in)
    q_sb = p[..., :C_SB_Q].reshape(b, length, SB_KV_HEADS, SB_REP, SB_HEAD_DIM)
    k_sb = p[..., C_SB_Q:C_SB_K].reshape(b, length, SB_KV_HEADS, SB_HEAD_DIM)
    v_sb = p[..., C_SB_K:C_SB_V].reshape(b, length, SB_KV_HEADS, SB_HEAD_DIM)
    c_q = rmsnorm(p[..., C_SB_V:C_CQ], cq_g)
    c_kv = rmsnorm(p[..., C_CQ:C_CKV], ckv_g)
    q = jnp.einsum('blc,chd->blhd', c_q, w_uq)
    q_nope = rmsnorm(q[..., :MLA_NOPE_DIM], qn_g)
    q_rope = rope(rmsnorm(q[..., MLA_NOPE_DIM:], qr_g), pos)
    k_rope = rope(rmsnorm(p[..., C_CKV:], kr_g), pos)
    return q_sb, k_sb, v_sb, q_nope, q_rope, c_kv, k_rope


def sb_prompt(q, k, v):
    b, length = q.shape[0], q.shape[1]
    outs = []
    for q0, q1 in query_blocks(length):
        z = jnp.einsum('bqgrd,bkgd->bgrqk', q[:, q0:q1], k[:, :q1]).astype(jnp.float32) * SB_SCALE
        vis = jnp.arange(q1)[None, :] < jnp.arange(q0, q1)[:, None]
        a = sb_weights(z, vis)
        outs.append(jnp.einsum('bgrqk,bkgd->bqgrd', a.astype(v.dtype), v[:, :q1]))
    return jnp.concatenate(outs, axis=1).reshape(b, length, SB_WIDTH)


def sb_sample(q, k_new, v_new, k_past, v_past):
    b, nq = q.shape[0], q.shape[1]
    n_past = k_past.shape[1]
    z_past = jnp.einsum('bqgrd,bpgd->bgrqp', q, k_past).astype(jnp.float32) * SB_SCALE
    z_new = jnp.einsum('bqgrd,bkgd->bgrqk', q, k_new).astype(jnp.float32) * SB_SCALE
    vis = jnp.concatenate([jnp.ones((nq, n_past), bool), jnp.tri(nq, nq, -1, dtype=bool)], axis=1)
    a = sb_weights(jnp.concatenate([z_past, z_new], axis=-1), vis).astype(v_new.dtype)
    o = (jnp.einsum('bgrqp,bpgd->bqgrd', a[..., :n_past], v_past)
         + jnp.einsum('bgrqk,bkgd->bqgrd', a[..., n_past:], v_new))
    return o.reshape(b, nq, SB_WIDTH)


def mla_prompt(q_nope_g, q_rope, c_kv, k_rope, w_uk, w_uv):
    b, length = q_nope_g.shape[0], q_nope_g.shape[1]
    k_raw = jnp.einsum('blr,rhd->blhd', c_kv, w_uk)
    k_inv = key_inv_rms(k_raw)
    v = jnp.einsum('blr,rhd->blhd', c_kv, w_uv)
    outs = []
    for q0, q1 in query_blocks(length):
        s = mla_scores(q_nope_g[:, q0:q1], q_rope[:, q0:q1], k_raw[:, :q1], k_inv[:, :q1], k_rope[:, :q1])
        vis = jnp.arange(q1)[None, :] <= jnp.arange(q0, q1)[:, None]
        pr = jax.nn.softmax(jnp.where(vis, s, -jnp.inf), axis=-1)
        outs.append(jnp.einsum('bhqk,bkhd->bqhd', pr.astype(v.dtype), v[:, :q1]))
    return jnp.concatenate(outs, axis=1).reshape(b, length, MLA_WIDTH)


def mla_sample(q_nope_g, q_rope, c_kv, k_rope, ckv_past, kr_past, w_uk, w_uv):
    b, nq = q_nope_g.shape[0], q_nope_g.shape[1]
    n_past = ckv_past.shape[1]
    kp_raw = jnp.einsum('bpr,rhd->bphd', ckv_past, w_uk)
    s_past = mla_scores(q_nope_g, q_rope, kp_raw, key_inv_rms(kp_raw), kr_past)
    kn_raw = jnp.einsum('bkr,rhd->bkhd', c_kv, w_uk)
    s_new = mla_scores(q_nope_g, q_rope, kn_raw, key_inv_rms(kn_raw), k_rope)
    s_new = jnp.where(jnp.tri(nq, nq, 0, dtype=bool), s_new, -jnp.inf)
    pr = jax.nn.softmax(jnp.concatenate([s_past, s_new], axis=-1), axis=-1).astype(c_kv.dtype)
    v_new = jnp.einsum('bkr,rhd->bkhd', c_kv, w_uv)
    lat = jnp.einsum('bhqp,bpr->bqhr', pr[..., :n_past], ckv_past)
    o = jnp.einsum('bqhr,rhd->bqhd', lat, w_uv) + jnp.einsum('bhqk,bkhd->bqhd', pr[..., n_past:], v_new)
    return o.reshape(b, nq, MLA_WIDTH)


def hier_moe(x, w_rg, b_rg, w_re, b_re, w_gate, w_up, w_down):
    t = x.reshape(-1, x.shape[-1])
    lg = jnp.dot(t, w_rg).astype(jnp.float32) + b_rg.astype(jnp.float32)
    g_idx = jnp.argmax(lg, axis=-1)
    p_grp = jnp.take_along_axis(jax.nn.softmax(lg, axis=-1), g_idx[:, None], axis=-1)
    le = (jnp.dot(t, w_re).astype(jnp.float32) + b_re.astype(jnp.float32)).reshape(-1, N_GROUPS, EXPERTS_PER_GROUP)
    le_sel = jnp.take_along_axis(le, g_idx[:, None, None], axis=1)[:, 0]
    top_v, top_i = lax.top_k(le_sel, TOP_K_INNER)
    w_k = p_grp * jax.nn.softmax(top_v, axis=-1)
    ids = g_idx[:, None] * EXPERTS_PER_GROUP + top_i
    combine = jnp.sum(jax.nn.one_hot(ids, N_EXPERTS, dtype=jnp.float32) * w_k[..., None], axis=1).astype(t.dtype)
    gate = jnp.einsum('td,edf->tef', t, w_gate)
    up = jnp.einsum('td,edf->tef', t, w_up)
    h = jax.nn.silu(gate) * up * combine[:, :, None]
    return jnp.einsum('tef,efd->td', h, w_down).reshape(x.shape)


def merge_and_channel_mix(x, sb_o, mla_o, g_sb, g_mla, w_o, norm2_g, w_rg, b_rg, w_re, b_re, w_gate, w_up, w_down):
    m = jnp.concatenate([rmsnorm(sb_o, g_sb), rmsnorm(mla_o, g_mla)], axis=-1)
    h = x + jnp.dot(m, w_o)
    return h + hier_moe(rmsnorm(h, norm2_g), w_rg, b_rg, w_re, b_re, w_gate, w_up, w_down)


def setup_inputs(seed: int = 0) -> dict:
    key = jax.random.key(seed)
    ks = iter(jax.random.split(key, 40))
    nrm = lambda shape, scale: jax.random.normal(next(ks), shape, jnp.float32) * scale
    gain = lambda shape: 1.0 + nrm(shape, 0.05)
    n_pages = PAST_LEN // PAGE_SIZE
    n_used = DEC_BATCH * n_pages
    n_pool = n_used + n_used // 4 + 1
    return {
        'x_prompt': nrm((BATCH, SEQ, D_MODEL), 1.0),
        'x_sample': nrm((DEC_BATCH, DEC_SEQ, D_MODEL), 1.0),
        'cache_sb_k': nrm((DEPTH, n_pool, PAGE_SIZE, SB_KV_HEADS, SB_HEAD_DIM), 1.0),
        'cache_sb_v': nrm((DEPTH, n_pool, PAGE_SIZE, SB_KV_HEADS, SB_HEAD_DIM), 1.0),
        'cache_mla_ckv': nrm((DEPTH, n_pool, PAGE_SIZE, MLA_KV_LORA), 1.0),
        'cache_mla_krope': nrm((DEPTH, n_pool, PAGE_SIZE, MLA_ROPE_DIM), 1.0),
        'page_table': jax.random.permutation(next(ks), n_pool)[:n_used].reshape(DEC_BATCH, n_pages).astype(jnp.int32),
        'meta_tokens': nrm((N_META, D_MODEL), 1.0),
        'norm1_g': gain((DEPTH, D_MODEL)),
        'w_in': nrm((DEPTH, D_MODEL, IN_COLS), D_MODEL ** -0.5),
        'cq_norm_g': gain((DEPTH, MLA_Q_LORA)),
        'ckv_norm_g': gain((DEPTH, MLA_KV_LORA)),
        'w_uq': nrm((DEPTH, MLA_Q_LORA, MLA_HEADS, MLA_NOPE_DIM + MLA_ROPE_DIM), MLA_Q_LORA ** -0.5),
        'qn_norm_g': gain((DEPTH, MLA_NOPE_DIM)),
        'qr_norm_g': gain((DEPTH, MLA_ROPE_DIM)),
        'kr_norm_g': gain((DEPTH, MLA_ROPE_DIM)),
        'w_uk': nrm((DEPTH, MLA_KV_LORA, MLA_HEADS, MLA_NOPE_DIM), MLA_KV_LORA ** -0.5),
        'kn_norm_g': gain((DEPTH, MLA_NOPE_DIM)),
        'w_uv': nrm((DEPTH, MLA_KV_LORA, MLA_HEADS, MLA_V_DIM), MLA_KV_LORA ** -0.5),
        'sb_out_norm_g': gain((DEPTH, SB_WIDTH)),
        'mla_out_norm_g': gain((DEPTH, MLA_WIDTH)),
        'w_o': nrm((DEPTH, MIX_WIDTH, D_MODEL), MIX_WIDTH ** -0.5),
        'norm2_g': gain((DEPTH, D_MODEL)),
        'w_router_group': nrm((DEPTH, D_MODEL, N_GROUPS), D_MODEL ** -0.5),
        'b_router_group': nrm((DEPTH, N_GROUPS), 0.01),
        'w_router_expert': nrm((DEPTH, D_MODEL, N_EXPERTS), D_MODEL ** -0.5),
        'b_router_expert': nrm((DEPTH, N_EXPERTS), 0.01),
        'w_gate': nrm((DEPTH, N_EXPERTS, D_MODEL, D_EXPERT), D_MODEL ** -0.5),
        'w_up': nrm((DEPTH, N_EXPERTS, D_MODEL, D_EXPERT), D_MODEL ** -0.5),
        'w_down': nrm((DEPTH, N_EXPERTS, D_EXPERT, D_MODEL), D_EXPERT ** -0.5),
    }


def reference(x_prompt, x_sample, cache_sb_k, cache_sb_v, cache_mla_ckv, cache_mla_krope, page_table,
              meta_tokens, norm1_g, w_in, cq_norm_g, ckv_norm_g, w_uq, qn_norm_g, qr_norm_g, kr_norm_g,
              w_uk, kn_norm_g, w_uv, sb_out_norm_g, mla_out_norm_g, w_o, norm2_g,
              w_router_group, b_router_group, w_router_expert, b_router_expert, w_gate, w_up, w_down):
    bp, _, d = x_prompt.shape
    xp = jnp.concatenate([jnp.broadcast_to(meta_tokens[None].astype(x_prompt.dtype), (bp, N_META, d)), x_prompt], axis=1)
    lp = xp.shape[1]
    bs, nq = x_sample.shape[0], x_sample.shape[1]
    n_past = page_table.shape[1] * cache_sb_k.shape[2]
    pos_p = jnp.arange(lp, dtype=jnp.int32)
    pos_s = n_past + jnp.arange(nq, dtype=jnp.int32)
    xs = x_sample
    sbk_p, sbv_p, ckv_p, kr_p = [], [], [], []
    sbk_s, sbv_s, ckv_s, kr_s = [], [], [], []
    for l in range(DEPTH):
        proj = (norm1_g[l], w_in[l], cq_norm_g[l], ckv_norm_g[l], w_uq[l], qn_norm_g[l], qr_norm_g[l], kr_norm_g[l])
        chan = (sb_out_norm_g[l], mla_out_norm_g[l], w_o[l], norm2_g[l], w_router_group[l], b_router_group[l],
                w_router_expert[l], b_router_expert[l], w_gate[l], w_up[l], w_down[l])
        q_sb, k_sb, v_sb, q_n, q_r, c_kv, k_r = mixer_inputs(xp, pos_p, *proj)
        sb_o = sb_prompt(q_sb, k_sb, v_sb)
        mla_o = mla_prompt(q_n * kn_norm_g[l], q_r, c_kv, k_r, w_uk[l], w_uv[l])
        xp = merge_and_channel_mix(xp, sb_o, mla_o, *chan)
        sbk_p.append(k_sb); sbv_p.append(v_sb); ckv_p.append(c_kv); kr_p.append(k_r)
        q_sb, k_sb, v_sb, q_n, q_r, c_kv, k_r = mixer_inputs(xs, pos_s, *proj)
        k_past = cache_sb_k[l, page_table].reshape(bs, n_past, SB_KV_HEADS, SB_HEAD_DIM)
        v_past = cache_sb_v[l, page_table].reshape(bs, n_past, SB_KV_HEADS, SB_HEAD_DIM)
        sb_o = sb_sample(q_sb, k_sb, v_sb, k_past, v_past)
        ckv_past = cache_mla_ckv[l, page_table].reshape(bs, n_past, MLA_KV_LORA)
        kr_past = cache_mla_krope[l, page_table].reshape(bs, n_past, MLA_ROPE_DIM)
        mla_o = mla_sample(q_n * kn_norm_g[l], q_r, c_kv, k_r, ckv_past, kr_past, w_uk[l], w_uv[l])
        xs = merge_and_channel_mix(xs, sb_o, mla_o, *chan)
        sbk_s.append(k_sb); sbv_s.append(v_sb); ckv_s.append(c_kv); kr_s.append(k_r)
    return (xp[:, N_META:], xs,
            jnp.stack(sbk_p), jnp.stack(sbv_p), jnp.stack(ckv_p), jnp.stack(kr_p),
            jnp.stack(sbk_s), jnp.stack(sbv_s), jnp.stack(ckv_s), jnp.stack(kr_s))
```

```python
import functools

import numpy as np
import jax
import jax.numpy as jnp
from jax import lax
from jax.experimental import pallas as pl
from jax.experimental.pallas import tpu as pltpu

F32 = jnp.float32
BF16 = jnp.bfloat16

D_MODEL = 1024
N_META = 16
EPS = 1e-6
SB_HEADS = 8
SB_KV_HEADS = 4
SB_HEAD_DIM = 64
SB_WIDTH = SB_HEADS * SB_HEAD_DIM
SB_SCALE = SB_HEAD_DIM ** -0.5
MLA_HEADS = 8
MLA_Q_LORA = 256
MLA_KV_LORA = 128
MLA_NOPE = 64
MLA_ROPE = 32
MLA_V = 64
MLA_WIDTH = MLA_HEADS * MLA_V
MLA_SCALE = (MLA_NOPE + MLA_ROPE) ** -0.5
ROPE_THETA = 10000.0
N_GROUPS = 4
EXPERTS_PER_GROUP = 8
N_EXPERTS = N_GROUPS * EXPERTS_PER_GROUP
D_EXPERT = 256

LANES = 128
BLK = 128
PAGE = 128
CHUNK_PAGES = 8
ROW_TILE = 256
EXPERT_TILE = 256
VMEM_LIMIT = 56 * 1024 * 1024
SB_DEAD_LOG = -104.0
NEG_BIG = -1e30

_IN_COLS = 512 + 256 + 256 + 256 + 128 + 128 + 128


def _full(shape):
    nd = len(shape)
    return pl.BlockSpec(shape, lambda *_: (0,) * nd)


def _split2(x):
    hi = x.astype(BF16)
    lo = (x - hi.astype(F32)).astype(BF16)
    return hi, lo


def _split3(x):
    hi = x.astype(BF16)
    r = x - hi.astype(F32)
    mid = r.astype(BF16)
    lo = (r - mid.astype(F32)).astype(BF16)
    return hi, mid, lo


def _dot(a, b):
    return jnp.dot(a, b, preferred_element_type=F32)


def _dot_nt(a, b):
    return lax.dot_general(a, b, (((1,), (1,)), ((), ())), preferred_element_type=F32)


def _softplus(z):
    return jnp.maximum(z, 0.0) + jnp.log1p(jnp.exp(-jnp.abs(z)))


def _proj_body(x_ref, cos_ref, sin_ref, n1g_ref, win_ref, cqg_ref, ckvg_ref, wa_ref, wb_ref,
               sq_ref, sqt_ref, g1_ref, g2_ref, g3_ref, gk1_ref, gk3_ref, wukp_ref, sk_ref,
               skt_ref, wuv_ref,
               qsb_ref, ksb_ref, vsb_ref, ksb16_ref, vsb16_ref, ckv_ref, kr_ref, qmla_ref,
               kmla_ref, vmla_ref):
    x = x_ref[...]
    xn = x * lax.rsqrt(jnp.mean(x * x, axis=-1, keepdims=True) + EPS) * n1g_ref[...]
    p = _dot(xn.astype(BF16), win_ref[...])
    qsb_ref[...] = p[:, 0:512]
    ksb = p[:, 512:768]
    vsb = p[:, 768:1024]
    ksb_ref[...] = ksb
    vsb_ref[...] = vsb
    ksb16_ref[...] = ksb.astype(BF16)
    vsb16_ref[...] = vsb.astype(BF16)

    cq = p[:, 1024:1280]
    cq = cq * lax.rsqrt(jnp.mean(cq * cq, axis=-1, keepdims=True) + EPS) * cqg_ref[...]
    ckv = p[:, 1280:1408]
    ckv = ckv * lax.rsqrt(jnp.mean(ckv * ckv, axis=-1, keepdims=True) + EPS) * ckvg_ref[...]
    ckv_ref[...] = ckv

    cos = cos_ref[...]
    sin = sin_ref[...]
    s1 = p[:, 1408:1536]
    s2 = p[:, 1536:1664]
    inv_kr = lax.rsqrt(jnp.sum(s1 * s1, axis=-1, keepdims=True) * (1.0 / MLA_ROPE) + EPS)
    kr = (s1 * (cos * gk1_ref[...]) + s2 * (sin * gk3_ref[...])) * inv_kr
    kr_ref[...] = kr

    cqb = cq.astype(BF16)
    a = _dot(cqb, wa_ref[...])
    b = _dot(cqb, wb_ref[...])
    hi, lo = _split2(a * a)
    msq = _dot(hi, sq_ref[...]) + _dot(lo, sq_ref[...])
    ihi, ilo = _split2(lax.rsqrt(msq + EPS))
    invf = _dot(ihi, sqt_ref[...]) + _dot(ilo, sqt_ref[...])

    ckvb = ckv.astype(BF16)
    kraw = _dot(ckvb, wukp_ref[...])
    khi, klo = _split2(kraw * kraw)
    kmsq = _dot(khi, sk_ref[...]) + _dot(klo, sk_ref[...])
    kihi, kilo = _split2(lax.rsqrt(kmsq + EPS))
    kinvf = _dot(kihi, skt_ref[...]) + _dot(kilo, skt_ref[...])
    kr_shift = pltpu.roll(kr, 64, axis=1)

    for h in range(MLA_HEADS):
        sl = slice(h * LANES, (h + 1) * LANES)
        qh = (a[:, sl] * (g1_ref[:, sl] + cos * g2_ref[:, sl])
              + b[:, sl] * (sin * g3_ref[:, sl])) * invf[:, sl]
        qmla_ref[:, sl] = qh.astype(BF16)
        kmla_ref[:, sl] = (kraw[:, sl] * kinvf[:, sl] + kr_shift).astype(BF16)
    vmla_ref[...] = _dot(ckvb, wuv_ref[...]).astype(BF16)


def _project(x2d, cos_t, sin_t, w, tm):
    t = x2d.shape[0]
    tbl_blocks = cos_t.shape[0] // tm
    row = lambda width: pl.BlockSpec((tm, width), lambda i: (i, 0))
    tbl = pl.BlockSpec((tm, LANES), lambda i: (i % tbl_blocks, 0))
    consts = [w['n1g'], w['win'], w['cqg'], w['ckvg'], w['wa'], w['wb'], w['sq'], w['sqt'],
              w['g1'], w['g2'], w['g3'], w['gk1'], w['gk3'], w['wukp'], w['sk'], w['skt'], w['wuv']]
    out_widths = [(512, F32), (256, F32), (256, F32), (256, BF16), (256, BF16), (128, F32),
                  (128, F32), (1024, BF16), (1024, BF16), (512, BF16)]
    outs = pl.pallas_call(
        _proj_body,
        grid=(t // tm,),
        in_specs=[row(D_MODEL), tbl, tbl] + [_full(c.shape) for c in consts],
        out_specs=[row(wd) for wd, _ in out_widths],
        out_shape=[jax.ShapeDtypeStruct((t, wd), dt) for wd, dt in out_widths],
        compiler_params=pltpu.CompilerParams(dimension_semantics=("arbitrary",),
                                             vmem_limit_bytes=VMEM_LIMIT),
        name="proj",
    )(x2d, cos_t, sin_t, *consts)
    names = ['qsb', 'ksb', 'vsb', 'ksb16', 'vsb16', 'ckv', 'kr', 'qmla', 'kmla', 'vmla']
    return dict(zip(names, outs))


def _sb_block(q2, kj, vj, vis, u2, carry_ref, acc_ref, k_is_transposed=False):
    z = _dot(q2, kj) if k_is_transposed else _dot_nt(q2, kj)
    sp = _softplus(z)
    log_rest = -sp
    if vis is not None:
        log_rest = jnp.where(vis, log_rest, 0.0)
    hi, mid, lo = _split3(log_rest)
    cs = _dot(hi, u2) + _dot(mid, u2) + _dot(lo, u2)
    later = cs[:, :LANES]
    total = cs[:, LANES:]
    c = carry_ref[...]
    wgt = jnp.exp(z - sp + later + c)
    if vis is not None:
        wgt = jnp.where(vis, wgt, 0.0)
    wb = wgt.astype(BF16)
    acc_ref[...] += _dot_nt(wb, vj) if k_is_transposed else _dot(wb, vj)
    carry_ref[...] = c + total


def _sb_prompt_body(q_ref, k_ref, v_ref, km_ref, vm_ref, u2_ref, o_ref, carry_ref, acc_ref):
    i = pl.program_id(1)
    lane = lax.broadcasted_iota(jnp.int32, (1, LANES), 1)
    lane_lo = lane < 64
    row_t = lax.broadcasted_iota(jnp.int32, (2 * BLK, 1), 0) % BLK
    vis_diag = lane < row_t
    vis_meta = jnp.broadcast_to(lane < N_META, (2 * BLK, LANES))
    u2 = u2_ref[...]
    for g in range(SB_KV_HEADS):
        pair = slice((g // 2) * LANES, (g // 2 + 1) * LANES)
        slab = q_ref[:, g * LANES:(g + 1) * LANES]
        rolled = pltpu.roll(slab, 64, axis=1)
        if g % 2 == 0:
            q_r0 = jnp.where(lane_lo, slab, 0.0)
            q_r1 = jnp.where(lane_lo, rolled, 0.0)
        else:
            q_r0 = jnp.where(lane_lo, 0.0, rolled)
            q_r1 = jnp.where(lane_lo, 0.0, slab)
        q2 = jnp.concatenate([q_r0, q_r1], axis=0).astype(BF16)
        carry_ref[...] = jnp.zeros_like(carry_ref)
        acc_ref[...] = jnp.zeros_like(acc_ref)

        def run(j, vis):
            rows = pl.ds(pl.multiple_of(j * BLK, BLK), BLK)
            _sb_block(q2, k_ref[rows, pair], v_ref[rows, pair], vis, u2, carry_ref, acc_ref)

        run(i, vis_diag)

        def cond(st):
            j, alive = st
            return jnp.logical_and(j >= 0, alive > SB_DEAD_LOG)

        def body(st):
            j, _ = st
            run(j, None)
            return j - 1, jnp.max(carry_ref[...])

        _, alive = lax.while_loop(cond, body, (i - 1, jnp.max(carry_ref[...])))

        @pl.when(alive > SB_DEAD_LOG)
        def _():
            _sb_block(q2, km_ref[:, pair], vm_ref[:, pair], vis_meta, u2, carry_ref, acc_ref)

        acc = acc_ref[...]
        o_r0 = acc[:BLK]
        o_r1 = acc[BLK:]
        if g % 2 == 0:
            out = jnp.where(lane_lo, o_r0, pltpu.roll(o_r1, 64, axis=1))
        else:
            out = jnp.where(lane_lo, pltpu.roll(o_r0, 64, axis=1), o_r1)
        o_ref[:, g * LANES:(g + 1) * LANES] = out


def _sb_prompt(qsb, ksb16, vsb16, km16, vm16, u2, batch, seq):
    nq = seq // BLK
    kv_w = SB_KV_HEADS * SB_HEAD_DIM
    return pl.pallas_call(
        _sb_prompt_body,
        grid=(batch, nq),
        in_specs=[pl.BlockSpec((BLK, SB_WIDTH), lambda b, i: (b * nq + i, 0)),
                  pl.BlockSpec((seq, kv_w), lambda b, i: (b, 0)),
                  pl.BlockSpec((seq, kv_w), lambda b, i: (b, 0)),
                  _full(km16.shape), _full(vm16.shape), _full(u2.shape)],
        out_specs=pl.BlockSpec((BLK, SB_WIDTH), lambda b, i: (b * nq + i, 0)),
        out_shape=jax.ShapeDtypeStruct((batch * seq, SB_WIDTH), F32),
        scratch_shapes=[pltpu.VMEM((2 * BLK, LANES), F32), pltpu.VMEM((2 * BLK, LANES), F32)],
        compiler_params=pltpu.CompilerParams(dimension_semantics=("arbitrary", "arbitrary"),
                                             vmem_limit_bytes=VMEM_LIMIT),
        name="sb_prompt",
    )(qsb, ksb16, vsb16, km16, vm16, u2)


def _mla_prompt_body(q_ref, k_ref, v_ref, km_ref, vm_ref, o_ref):
    i = pl.program_id(1)
    lane = lax.broadcasted_iota(jnp.int32, (1, LANES), 1)
    row_t = lax.broadcasted_iota(jnp.int32, (BLK, 1), 0)
    vis_diag = lane <= row_t
    vis_meta = jnp.broadcast_to(lane < N_META, (BLK, LANES))
    lane_lo = lane < 64

    def step(qh, kj, vj, vis, st):
        m, l, acc = st
        s = _dot_nt(qh, kj)
        if vis is not None:
            s = jnp.where(vis, s, NEG_BIG)
        m_new = jnp.maximum(m, jnp.max(s, axis=-1, keepdims=True))
        p = jnp.exp(s - m_new)
        alpha = jnp.exp(m - m_new)
        l = alpha * l + jnp.sum(p, axis=-1, keepdims=True)
        acc = alpha * acc + _dot(p.astype(BF16), vj)
        return m_new, l, acc

    for hp in range(MLA_HEADS // 2):
        vpair = slice(hp * LANES, (hp + 1) * LANES)
        res = []
        for h in (2 * hp, 2 * hp + 1):
            hs = slice(h * LANES, (h + 1) * LANES)
            qh = q_ref[:, hs]
            st = (jnp.full((BLK, 1), NEG_BIG, F32), jnp.zeros((BLK, 1), F32),
                  jnp.zeros((BLK, LANES), F32))
            st = step(qh, km_ref[:, hs], vm_ref[:, vpair], vis_meta, st)

            def body(j, st, qh=qh, hs=hs):
                rows = pl.ds(pl.multiple_of(j * BLK, BLK), BLK)
                return step(qh, k_ref[rows, hs], v_ref[rows, vpair], None, st)

            st = lax.fori_loop(0, i, body, st)
            rows = pl.ds(pl.multiple_of(i * BLK, BLK), BLK)
            m, l, acc = step(qh, k_ref[rows, hs], v_ref[rows, vpair], vis_diag, st)
            res.append(acc / l)
        o_ref[:, vpair] = jnp.where(lane_lo, res[0], res[1])


def _mla_prompt(qmla, kmla, vmla, kmm, vmm, batch, seq):
    nq = seq // BLK
    return pl.pallas_call(
        _mla_prompt_body,
        grid=(batch, nq),
        in_specs=[pl.BlockSpec((BLK, MLA_HEADS * LANES), lambda b, i: (b * nq + i, 0)),
                  pl.BlockSpec((seq, MLA_HEADS * LANES), lambda b, i: (b, 0)),
                  pl.BlockSpec((seq, MLA_WIDTH), lambda b, i: (b, 0)),
                  _full(kmm.shape), _full(vmm.shape)],
        out_specs=pl.BlockSpec((BLK, MLA_WIDTH), lambda b, i: (b * nq + i, 0)),
        out_shape=jax.ShapeDtypeStruct((batch * seq, MLA_WIDTH), F32),
        compiler_params=pltpu.CompilerParams(dimension_semantics=("arbitrary", "arbitrary"),
                                             vmem_limit_bytes=VMEM_LIMIT),
        name="mla_prompt",
    )(qmla, kmla, vmla, kmm, vmm)


def _sample_body(pt_ref,
                 kT_hbm, vT_hbm, ckv_hbm, krT_hbm,
                 qsb_ref, ksbn_ref, vsbn_ref, qbd_ref, kmn_ref, vmn_ref,
                 wukt_ref, wabs_ref, shs_ref, wuv_ref, u2_ref,
                 sbo_ref, mlao_ref,
                 cbuf, rbuf, sbk, sbv, wq_ref, qr_ref, m_ref, l_ref, lat_ref,
                 carry_ref, sbacc_ref, sem_c, sem_r, sem_sb):
    b = pl.program_id(0)
    c = pl.program_id(1)
    nb = pl.num_programs(0)
    nc = pl.num_programs(1)
    n_pages = nc * CHUNK_PAGES
    step = b * nc + c
    slot = step % 2
    chunk = CHUNK_PAGES * PAGE
    rows_q = MLA_HEADS * 4

    def chunk_copies(bb, cc, sl):
        cps = []
        for pg in range(CHUNK_PAGES):
            page = pt_ref[bb, cc * CHUNK_PAGES + pg]
            cps.append(pltpu.make_async_copy(
                ckv_hbm.at[0, page], cbuf.at[sl, pl.ds(pg * PAGE, PAGE), :], sem_c.at[sl]))
            cps.append(pltpu.make_async_copy(
                krT_hbm.at[0, page], rbuf.at[sl, :, pl.ds(pg * PAGE, PAGE)], sem_r.at[sl]))
        return cps

    def sb_copies(bb, page_idx, sl):
        page = pt_ref[bb, page_idx]
        return [pltpu.make_async_copy(kT_hbm.at[0, page], sbk.at[sl], sem_sb.at[sl, 0]),
                pltpu.make_async_copy(vT_hbm.at[0, page], sbv.at[sl], sem_sb.at[sl, 1])]

    @pl.when(step == 0)
    def _():
        for cp in chunk_copies(0, 0, 0):
            cp.start()

    @pl.when(c == 0)
    def _():
        for sl in range(2):
            for cp in sb_copies(b, n_pages - 1 - sl, sl):
                cp.start()
        qa = _dot(qbd_ref[0], wabs_ref[...])
        wq_ref[0:512, :] = wukt_ref[...]
        wq_ref[512:512 + rows_q, :] = qa[:, :LANES].astype(BF16)
        qr_ref[...] = qa[:, LANES:LANES + MLA_ROPE].astype(BF16)
        m_ref[...] = jnp.full_like(m_ref, NEG_BIG)
        l_ref[...] = jnp.zeros_like(l_ref)
        lat_ref[...] = jnp.zeros_like(lat_ref)

    for cp in chunk_copies(b, c, slot):
        cp.wait()

    @pl.when(step + 1 < nb * nc)
    def _():
        nxt = step + 1
        for cp in chunk_copies(nxt // nc, nxt % nc, 1 - slot):
            cp.start()

    cb = cbuf[slot].astype(BF16)
    big = _dot_nt(wq_ref[...], cb)
    k2 = big[0:512] * big[0:512]
    part = k2.reshape(MLA_HEADS, MLA_NOPE // 8, 8, chunk).sum(axis=1).reshape(MLA_HEADS * 8, chunk)
    ksum = _dot(shs_ref[...], part.astype(BF16))
    s = big[512:512 + rows_q] * lax.rsqrt(ksum + EPS) + _dot(qr_ref[...], rbuf[slot].astype(BF16))
    m_old = m_ref[...]
    m_new = jnp.maximum(m_old, jnp.max(s, axis=-1, keepdims=True))
    p = jnp.exp(s - m_new)
    alpha = jnp.exp(m_old - m_new)
    l_ref[...] = alpha * l_ref[...] + jnp.sum(p, axis=-1, keepdims=True)
    lat_ref[...] = alpha * lat_ref[...] + _dot(p.astype(BF16), cb)
    m_ref[...] = m_new

    @pl.when(c == nc - 1)
    def _():
        lane8 = lax.broadcasted_iota(jnp.int32, (rows_q, 16), 1)
        q_of_row = lax.broadcasted_iota(jnp.int32, (rows_q, 16), 0) % 4
        s_new = _dot_nt(qbd_ref[0], kmn_ref[0])
        s_new = jnp.where(lane8 <= q_of_row, s_new, NEG_BIG)
        m_old = m_ref[...]
        m_fin = jnp.maximum(m_old, jnp.max(s_new, axis=-1, keepdims=True))
        p_new = jnp.exp(s_new - m_fin)
        alpha = jnp.exp(m_old - m_fin)
        l_fin = alpha * l_ref[...] + jnp.sum(p_new, axis=-1, keepdims=True)
        o = _dot((alpha * lat_ref[...]).astype(BF16), wuv_ref[...]) + _dot(p_new.astype(BF16), vmn_ref[0])
        mlao_ref[0] = o / l_fin

        u2 = u2_ref[...]
        lane = lax.broadcasted_iota(jnp.int32, (1, LANES), 1)
        q_row = lax.broadcasted_iota(jnp.int32, (8, 1), 0) % 4
        vis_new = lane < q_row
        zpad = jnp.zeros((PAGE - 8, SB_HEAD_DIM), F32)
        for g in range(SB_KV_HEADS):
            qg = qsb_ref[0, g].astype(BF16)
            rows = slice(g * 8, (g + 1) * 8)
            carry = carry_ref.at[rows]
            acc = sbacc_ref.at[rows]
            carry[...] = jnp.zeros((8, LANES), F32)
            acc[...] = jnp.zeros((8, SB_HEAD_DIM), F32)
            kn = jnp.concatenate([ksbn_ref[0, g], zpad], axis=0).astype(BF16)
            vn = jnp.concatenate([vsbn_ref[0, g], zpad], axis=0).astype(BF16)
            _sb_block(qg, kn, vn, vis_new, u2, carry, acc)

        def sb_page(sl):
            for g in range(SB_KV_HEADS):
                qg = qsb_ref[0, g].astype(BF16)
                rows = slice(g * 8, (g + 1) * 8)
                _sb_block(qg, sbk[sl, g].astype(BF16), sbv[sl, g].astype(BF16), None, u2,
                          carry_ref.at[rows], sbacc_ref.at[rows], k_is_transposed=True)

        for sl in range(2):
            for cp in sb_copies(b, n_pages - 1 - sl, sl):
                cp.wait()
            sb_page(sl)

        def cond(st):
            j, alive = st
            return jnp.logical_and(j >= 0, alive > SB_DEAD_LOG)

        def body(st):
            j, _ = st
            cps = sb_copies(b, j, 0)
            for cp in cps:
                cp.start()
            for cp in cps:
                cp.wait()
            sb_page(0)
            return j - 1, jnp.max(carry_ref[...])

        lax.while_loop(cond, body, (n_pages - 3, jnp.max(carry_ref[...])))
        for g in range(SB_KV_HEADS):
            sbo_ref[0, g] = sbacc_ref[g * 8:(g + 1) * 8, :]


def _sample_attention(page_table, kT, vT, ckv_cache, krT, qsb_g, ksbn_g, vsbn_g, qbd, kmn, vmn, w):
    nb, n_pages = page_table.shape
    nc = n_pages // CHUNK_PAGES
    chunk = CHUNK_PAGES * PAGE
    rows_q = MLA_HEADS * 4
    per_b = lambda shape: pl.BlockSpec((1,) + shape, lambda b, c, pt: (b,) + (0,) * len(shape))
    const = lambda a: pl.BlockSpec(a.shape, lambda b, c, pt: (0,) * a.ndim)
    consts = [w['wukt'], w['wabs'], w['shs'], w['wuv'], w['u2']]
    any_spec = pl.BlockSpec(memory_space=pl.ANY)
    return pl.pallas_call(
        _sample_body,
        grid_spec=pltpu.PrefetchScalarGridSpec(
            num_scalar_prefetch=1,
            grid=(nb, nc),
            in_specs=[any_spec, any_spec, any_spec, any_spec,
                      per_b((SB_KV_HEADS, 8, SB_HEAD_DIM)), per_b((SB_KV_HEADS, 8, SB_HEAD_DIM)),
                      per_b((SB_KV_HEADS, 8, SB_HEAD_DIM)), per_b((rows_q, MLA_HEADS * LANES)),
                      per_b((16, MLA_HEADS * LANES)), per_b((16, MLA_WIDTH))]
                     + [const(a) for a in consts],
            out_specs=[per_b((SB_KV_HEADS, 8, SB_HEAD_DIM)), per_b((rows_q, MLA_WIDTH))],
            scratch_shapes=[
                pltpu.VMEM((2, chunk, MLA_KV_LORA), F32),
                pltpu.VMEM((2, MLA_ROPE, chunk), F32),
                pltpu.VMEM((2, SB_KV_HEADS, SB_HEAD_DIM, PAGE), F32),
                pltpu.VMEM((2, SB_KV_HEADS, SB_HEAD_DIM, PAGE), F32),
                pltpu.VMEM((512 + rows_q, MLA_KV_LORA), BF16),
                pltpu.VMEM((rows_q, MLA_ROPE), BF16),
                pltpu.VMEM((rows_q, 1), F32),
                pltpu.VMEM((rows_q, 1), F32),
                pltpu.VMEM((rows_q, MLA_KV_LORA), F32),
                pltpu.VMEM((SB_KV_HEADS * 8, LANES), F32),
                pltpu.VMEM((SB_KV_HEADS * 8, SB_HEAD_DIM), F32),
                pltpu.SemaphoreType.DMA((2,)),
                pltpu.SemaphoreType.DMA((2,)),
                pltpu.SemaphoreType.DMA((2, 2)),
            ]),
        out_shape=[jax.ShapeDtypeStruct((nb, SB_KV_HEADS, 8, SB_HEAD_DIM), F32),
                   jax.ShapeDtypeStruct((nb, rows_q, MLA_WIDTH), F32)],
        compiler_params=pltpu.CompilerParams(dimension_semantics=("arbitrary", "arbitrary"),
                                             vmem_limit_bytes=VMEM_LIMIT),
        name="sample_attn",
    )(page_table, kT, vT, ckv_cache, krT, qsb_g, ksbn_g, vsbn_g, qbd, kmn, vmn, *consts)


def _post_body(sbo_ref, mlao_ref, x_ref, gsb_ref, gmla_ref, wo_ref, n2g_ref, wrh_ref, wrl_ref,
               br_ref, h_ref, t_ref, ids_ref, wts_ref):
    sbo = sbo_ref[...]
    mlao = mlao_ref[...]
    m_sb = sbo * lax.rsqrt(jnp.mean(sbo * sbo, axis=-1, keepdims=True) + EPS) * gsb_ref[...]
    m_mla = mlao * lax.rsqrt(jnp.mean(mlao * mlao, axis=-1, keepdims=True) + EPS) * gmla_ref[...]
    h = (x_ref[...] + _dot(m_sb.astype(BF16), wo_ref[0:SB_WIDTH, :])
         + _dot(m_mla.astype(BF16), wo_ref[SB_WIDTH:, :]))
    h_ref[...] = h
    t = h * lax.rsqrt(jnp.mean(h * h, axis=-1, keepdims=True) + EPS) * n2g_ref[...]
    t_ref[...] = t
    thi, tlo = _split2(t)
    lg = (_dot(thi, wrh_ref[...]) + _dot(tlo, wrh_ref[...]) + _dot(thi, wrl_ref[...])
          + br_ref[...])
    tm = lg.shape[0]
    lane_i = lax.broadcasted_iota(jnp.int32, (tm, LANES), 1)
    lane = lane_i.astype(F32)
    big_i = jnp.float32(1 << 20)
    is_grp = lane < N_GROUPS
    mg = jnp.max(jnp.where(is_grp, lg, -jnp.inf), axis=-1, keepdims=True)
    gidx = jnp.min(jnp.where(is_grp & (lg == mg), lane, big_i), axis=-1, keepdims=True)
    p_grp = 1.0 / jnp.sum(jnp.where(is_grp, jnp.exp(lg - mg), 0.0), axis=-1, keepdims=True)
    lo_lane = N_GROUPS + gidx * EXPERTS_PER_GROUP
    sel = (lane >= lo_lane) & (lane < lo_lane + EXPERTS_PER_GROUP)
    v1 = jnp.max(jnp.where(sel, lg, -jnp.inf), axis=-1, keepdims=True)
    i1 = jnp.min(jnp.where(sel & (lg == v1), lane, big_i), axis=-1, keepdims=True)
    sel2 = sel & (lane != i1)
    v2 = jnp.max(jnp.where(sel2, lg, -jnp.inf), axis=-1, keepdims=True)
    i2 = jnp.min(jnp.where(sel2 & (lg == v2), lane, big_i), axis=-1, keepdims=True)
    e21 = jnp.exp(v2 - v1)
    w1 = p_grp / (1.0 + e21)
    w2 = w1 * e21
    ids = jnp.where(lane_i == 0, i1 - N_GROUPS, jnp.where(lane_i == 1, i2 - N_GROUPS, 0.0))
    ids_ref[...] = ids.astype(jnp.int32)
    wts_ref[...] = jnp.where(lane_i == 0, w1, jnp.where(lane_i == 1, w2, 0.0))


def _post_attention(sbo, mlao, x2d, w, tm):
    t = x2d.shape[0]
    row = lambda width: pl.BlockSpec((tm, width), lambda i: (i, 0))
    consts = [w['gsb'], w['gmla'], w['wo'], w['n2g'], w['wrh'], w['wrl'], w['br']]
    return pl.pallas_call(
        _post_body,
        grid=(t // tm,),
        in_specs=[row(SB_WIDTH), row(MLA_WIDTH), row(D_MODEL)] + [_full(c.shape) for c in consts],
        out_specs=[row(D_MODEL), row(D_MODEL), row(LANES), row(LANES)],
        out_shape=[jax.ShapeDtypeStruct((t, D_MODEL), F32), jax.ShapeDtypeStruct((t, D_MODEL), F32),
                   jax.ShapeDtypeStruct((t, LANES), jnp.int32), jax.ShapeDtypeStruct((t, LANES), F32)],
        compiler_params=pltpu.CompilerParams(dimension_semantics=("arbitrary",),
                                             vmem_limit_bytes=VMEM_LIMIT),
        name="post_attn",
    )(sbo, mlao, x2d, *consts)


def _wait_rows(src, dst, sem, cnt):
    def wait8(_, c):
        pltpu.make_async_copy(src.at[pl.ds(0, 8), :], dst.at[pl.ds(0, 8), :], sem).wait()
        return c

    def wait1(_, c):
        pltpu.make_async_copy(src.at[pl.ds(0, 1), :], dst.at[pl.ds(0, 1), :], sem).wait()
        return c

    lax.fori_loop(0, cnt // 8, wait8, 0)
    lax.fori_loop(0, cnt % 8, wait1, 0)


def _moe_body(tile_e_ref, tile_cnt_ref, row_src_ref, row_dst_ref,
              t_hbm, roww_ref, wg_ref, wu_ref, wd_ref, y_hbm, xbuf, ybuf, sem_in, sem_out):
    n = pl.program_id(0)
    cnt = tile_cnt_ref[n]

    @pl.when(n == 0)
    def _():
        xbuf[...] = jnp.zeros_like(xbuf)

    @pl.when(cnt > 0)
    def _():
        base = n * EXPERT_TILE

        def gather(r, _):
            pltpu.make_async_copy(t_hbm.at[pl.ds(row_src_ref[base + r], 1), :],
                                  xbuf.at[pl.ds(r, 1), :], sem_in).start()
            return 0

        lax.fori_loop(0, cnt, gather, 0)
        _wait_rows(t_hbm, xbuf, sem_in, cnt)
        xb = xbuf[...].astype(BF16)
        gate = _dot(xb, wg_ref[0].astype(BF16))
        up = _dot(xb, wu_ref[0].astype(BF16))
        hid = gate * jax.nn.sigmoid(gate) * up * roww_ref[...]
        ybuf[...] = _dot(hid.astype(BF16), wd_ref[0].astype(BF16))

        def scatter(r, _):
            pltpu.make_async_copy(ybuf.at[pl.ds(r, 1), :],
                                  y_hbm.at[pl.ds(row_dst_ref[base + r], 1), :], sem_out).start()
            return 0

        lax.fori_loop(0, cnt, scatter, 0)
        _wait_rows(ybuf, y_hbm, sem_out, cnt)


def _moe(t2d, ids, wts, w_gate, w_up, w_down):
    t = t2d.shape[0]
    n_assign = 2 * t
    n_tiles = n_assign // EXPERT_TILE + N_EXPERTS
    flat_e = ids.reshape(-1)
    flat_w = wts.reshape(-1)
    order = jnp.argsort(flat_e, stable=True).astype(jnp.int32)
    counts = jnp.sum((flat_e[:, None] == jnp.arange(N_EXPERTS)[None, :]).astype(jnp.int32), axis=0)
    tiles_e = (counts + EXPERT_TILE - 1) // EXPERT_TILE
    tile_end = jnp.cumsum(tiles_e)
    tile_start = tile_end - tiles_e
    seg_start = jnp.cumsum(counts) - counts
    tile_ids = jnp.arange(n_tiles, dtype=jnp.int32)
    tile_e = jnp.minimum(jnp.searchsorted(tile_end, tile_ids, side='right'), N_EXPERTS - 1).astype(jnp.int32)
    pos = jnp.arange(n_tiles * EXPERT_TILE, dtype=jnp.int32)
    pe = tile_e[pos // EXPERT_TILE]
    r = pos - tile_start[pe] * EXPERT_TILE
    valid = (r < counts[pe]) & (pos // EXPERT_TILE < tile_end[-1])
    a = order[jnp.clip(seg_start[pe] + r, 0, n_assign - 1)]
    row_src = jnp.where(valid, a // 2, 0).astype(jnp.int32)
    row_dst = jnp.where(valid, (a % 2) * t + a // 2, 0).astype(jnp.int32)
    row_w = jnp.where(valid, flat_w[a], 0.0).astype(F32)[:, None]
    tile_cnt = jnp.where(tile_ids < tile_end[-1],
                         jnp.clip(counts[tile_e] - (tile_ids - tile_start[tile_e]) * EXPERT_TILE,
                                  0, EXPERT_TILE), 0).astype(jnp.int32)
    wspec = lambda shape: pl.BlockSpec((1,) + shape, lambda n, te, nv, rs, rd: (te[n], 0, 0))
    return pl.pallas_call(
        _moe_body,
        grid_spec=pltpu.PrefetchScalarGridSpec(
            num_scalar_prefetch=4,
            grid=(n_tiles,),
            in_specs=[pl.BlockSpec(memory_space=pl.ANY),
                      pl.BlockSpec((EXPERT_TILE, 1), lambda n, te, nv, rs, rd: (n, 0)),
                      wspec((D_MODEL, D_EXPERT)), wspec((D_MODEL, D_EXPERT)), wspec((D_EXPERT, D_MODEL))],
            out_specs=pl.BlockSpec(memory_space=pl.ANY),
            scratch_shapes=[pltpu.VMEM((EXPERT_TILE, D_MODEL), F32), pltpu.VMEM((EXPERT_TILE, D_MODEL), F32),
                            pltpu.SemaphoreType.DMA(()), pltpu.SemaphoreType.DMA(())]),
        out_shape=jax.ShapeDtypeStruct((n_assign, D_MODEL), F32),
        compiler_params=pltpu.CompilerParams(dimension_semantics=("arbitrary",),
                                             vmem_limit_bytes=VMEM_LIMIT),
        name="moe",
    )(tile_e, tile_cnt, row_src, row_dst, t2d, row_w, w_gate, w_up, w_down)


def _combine_body(h_ref, y0_ref, y1_ref, o_ref):
    o_ref[...] = h_ref[...] + y0_ref[...] + y1_ref[...]


def _combine(h2d, y, tm):
    t = h2d.shape[0]
    nblk = t // tm
    return pl.pallas_call(
        _combine_body,
        grid=(nblk,),
        in_specs=[pl.BlockSpec((tm, D_MODEL), lambda i: (i, 0)),
                  pl.BlockSpec((tm, D_MODEL), lambda i: (i, 0)),
                  pl.BlockSpec((tm, D_MODEL), lambda i: (i + nblk, 0))],
        out_specs=pl.BlockSpec((tm, D_MODEL), lambda i: (i, 0)),
        out_shape=jax.ShapeDtypeStruct((t, D_MODEL), F32),
        name="moe_combine",
    )(h2d, y, y)


def _rot_half_cols(wr):
    half = MLA_ROPE // 2
    return jnp.concatenate([-wr[..., half:], wr[..., :half]], axis=-1)


def _swap_halves(g):
    half = MLA_ROPE // 2
    return jnp.concatenate([g[..., half:], g[..., :half]], axis=-1)


def _group_sum_consts():
    sq = np.zeros((MLA_HEADS * LANES, LANES), np.float32)
    sqt = np.zeros((LANES, MLA_HEADS * LANES), np.float32)
    sk = np.zeros((MLA_HEADS * LANES, LANES), np.float32)
    skt = np.zeros((LANES, MLA_HEADS * LANES), np.float32)
    for h in range(MLA_HEADS):
        nope = slice(h * LANES, h * LANES + MLA_NOPE)
        rope = slice(h * LANES + MLA_NOPE, h * LANES + MLA_NOPE + MLA_ROPE)
        sq[nope, 2 * h] = 1.0 / MLA_NOPE
        sq[rope, 2 * h + 1] = 1.0 / MLA_ROPE
        sqt[2 * h, nope] = 1.0
        sqt[2 * h + 1, rope] = 1.0
        sk[nope, h] = 1.0 / MLA_NOPE
        skt[h, nope] = 1.0
    u2 = np.zeros((LANES, 2 * LANES), np.float32)
    u2[:, :LANES] = (np.arange(LANES)[:, None] > np.arange(LANES)[None, :]).astype(np.float32)
    u2[:, LANES:] = 1.0
    shs = np.zeros((MLA_HEADS * 4, MLA_HEADS * 8), np.float32)
    for h in range(MLA_HEADS):
        shs[h * 4:(h + 1) * 4, h * 8:(h + 1) * 8] = 1.0 / MLA_NOPE
    as16 = lambda a: jnp.asarray(a, dtype=BF16)
    return dict(sq=as16(sq), sqt=as16(sqt), sk=as16(sk), skt=as16(skt), u2=as16(u2), shs=as16(shs))


def _prep_weights(norm1_g, w_in, cq_norm_g, ckv_norm_g, w_uq, qn_norm_g, qr_norm_g, kr_norm_g,
                  w_uk, kn_norm_g, w_uv, sb_out_norm_g, mla_out_norm_g, w_o, norm2_g,
                  w_router_group, b_router_group, w_router_expert, b_router_expert):
    w = _group_sum_consts()
    wr = w_in[:, 1408:1440]
    pad96 = jnp.zeros((D_MODEL, LANES - MLA_ROPE), F32)
    w['win'] = jnp.concatenate([w_in[:, :512] * SB_SCALE, w_in[:, 512:1408], wr, pad96,
                                _rot_half_cols(wr), pad96], axis=1).astype(BF16)
    w['n1g'] = norm1_g[None, :]
    w['cqg'] = cq_norm_g[None, :]
    w['ckvg'] = ckv_norm_g[None, :]
    w['wa'] = jnp.pad(w_uq, ((0, 0), (0, 0), (0, LANES - MLA_NOPE - MLA_ROPE))).reshape(
        MLA_Q_LORA, MLA_HEADS * LANES).astype(BF16)
    w['wb'] = jnp.pad(_rot_half_cols(w_uq[:, :, MLA_NOPE:]),
                      ((0, 0), (0, 0), (MLA_NOPE, LANES - MLA_NOPE - MLA_ROPE))).reshape(
        MLA_Q_LORA, MLA_HEADS * LANES).astype(BF16)
    z32 = jnp.zeros((LANES - MLA_NOPE - MLA_ROPE,), F32)
    z64 = jnp.zeros((MLA_NOPE,), F32)
    z96 = jnp.zeros((LANES - MLA_ROPE,), F32)
    head_row = lambda v: jnp.tile(v, MLA_HEADS)[None, :]
    w['g1'] = head_row(jnp.concatenate([qn_norm_g * kn_norm_g * MLA_SCALE, jnp.zeros((64,), F32)]))
    w['g2'] = head_row(jnp.concatenate([z64, qr_norm_g * MLA_SCALE, z32]))
    w['g3'] = head_row(jnp.concatenate([z64, _swap_halves(qr_norm_g) * MLA_SCALE, z32]))
    w['gk1'] = jnp.concatenate([kr_norm_g, z96])[None, :]
    w['gk3'] = jnp.concatenate([_swap_halves(kr_norm_g), z96])[None, :]
    w['wukp'] = jnp.pad(w_uk, ((0, 0), (0, 0), (0, LANES - MLA_NOPE))).reshape(
        MLA_KV_LORA, MLA_HEADS * LANES).astype(BF16)
    w['wuv'] = w_uv.reshape(MLA_KV_LORA, MLA_WIDTH).astype(BF16)
    w['wukt'] = w_uk.reshape(MLA_KV_LORA, MLA_HEADS * MLA_NOPE).T.astype(BF16)
    wabs_nope = jnp.pad(jnp.transpose(w_uk, (1, 2, 0)), ((0, 0), (0, LANES - MLA_NOPE), (0, 0)))
    e_r = np.zeros((MLA_HEADS, LANES, LANES), np.float32)
    for j in range(MLA_ROPE):
        e_r[:, MLA_NOPE + j, j] = 1.0
    w['wabs'] = jnp.concatenate([wabs_nope, jnp.asarray(e_r)], axis=-1).reshape(
        MLA_HEADS * LANES, 2 * LANES).astype(BF16)
    w['gsb'] = sb_out_norm_g[None, :]
    w['gmla'] = mla_out_norm_g[None, :]
    w['wo'] = w_o.astype(BF16)
    w['n2g'] = norm2_g[None, :]
    wr_all = jnp.pad(jnp.concatenate([w_router_group, w_router_expert], axis=1),
                     ((0, 0), (0, LANES - N_GROUPS - N_EXPERTS)))
    w['wrh'], w['wrl'] = _split2(wr_all)
    w['br'] = jnp.pad(jnp.concatenate([b_router_group, b_router_expert]),
                      (0, LANES - N_GROUPS - N_EXPERTS))[None, :]
    return w


def _rope_tables(pos):
    half = MLA_ROPE // 2
    inv_freq = ROPE_THETA ** (-jnp.arange(half, dtype=F32) / half)
    ang = pos.astype(F32)[:, None] * inv_freq[None, :]
    z = jnp.zeros((pos.shape[0], 32), F32)

    def slab(v):
        return jnp.concatenate([v, v, z, v, v, z], axis=1)

    return slab(jnp.cos(ang)), slab(jnp.sin(ang))


def _pad_rows(a, rows):
    return jnp.pad(a, ((0, rows - a.shape[0]), (0, 0)))


def _kv_out(a, batch, length, tail):
    return a.reshape((1, batch, length) + tail)


def kernel(x_prompt, x_sample, cache_sb_k, cache_sb_v, cache_mla_ckv, cache_mla_krope, page_table, meta_tokens, norm1_g, w_in, cq_norm_g, ckv_norm_g, w_uq, qn_norm_g, qr_norm_g, kr_norm_g, w_uk, kn_norm_g, w_uv, sb_out_norm_g, mla_out_norm_g, w_o, norm2_g, w_router_group, b_router_group, w_router_expert, b_router_expert, w_gate, w_up, w_down):
    batch, seq, _ = x_prompt.shape
    nb, nq, _ = x_sample.shape
    n_past = page_table.shape[1] * cache_sb_k.shape[2]
    assert nq == 4 and seq % BLK == 0 and page_table.shape[1] % CHUNK_PAGES == 0
    w = _prep_weights(norm1_g[0], w_in[0], cq_norm_g[0], ckv_norm_g[0], w_uq[0], qn_norm_g[0],
                      qr_norm_g[0], kr_norm_g[0], w_uk[0], kn_norm_g[0], w_uv[0], sb_out_norm_g[0],
                      mla_out_norm_g[0], w_o[0], norm2_g[0], w_router_group[0], b_router_group[0],
                      w_router_expert[0], b_router_expert[0])
    wg, wu, wd = w_gate[0], w_up[0], w_down[0]

    cos_m, sin_m = _rope_tables(jnp.arange(N_META))
    cos_p, sin_p = _rope_tables(N_META + jnp.arange(seq))
    cos_s, sin_s = _rope_tables(jnp.tile(n_past + jnp.arange(nq), nb))
    xp2 = x_prompt.reshape(batch * seq, D_MODEL)
    xs2 = x_sample.reshape(nb * nq, D_MODEL)
    tm_s = min(ROW_TILE, nb * nq)
    pm = _project(meta_tokens, cos_m, sin_m, w, N_META)
    pp = _project(xp2, cos_p, sin_p, w, ROW_TILE)
    ps = _project(xs2, cos_s, sin_s, w, tm_s)

    sbo_p = _sb_prompt(pp['qsb'], pp['ksb16'], pp['vsb16'], _pad_rows(pm['ksb16'], BLK),
                       _pad_rows(pm['vsb16'], BLK), w['u2'], batch, seq)
    mlao_p = _mla_prompt(pp['qmla'], pp['kmla'], pp['vmla'], _pad_rows(pm['kmla'], BLK),
                         _pad_rows(pm['vmla'], BLK), batch, seq)

    kT = jnp.transpose(cache_sb_k, (0, 1, 3, 4, 2))
    vT = jnp.transpose(cache_sb_v, (0, 1, 3, 4, 2))
    krT = jnp.transpose(cache_mla_krope, (0, 1, 3, 2))
    qsb_g = ps['qsb'].reshape(nb, nq, SB_KV_HEADS, 2, SB_HEAD_DIM).transpose(0, 2, 3, 1, 4).reshape(
        nb, SB_KV_HEADS, 8, SB_HEAD_DIM)
    pad_keys = lambda a: jnp.pad(a.reshape(nb, nq, SB_KV_HEADS, SB_HEAD_DIM).transpose(0, 2, 1, 3),
                                 ((0, 0), (0, 0), (0, 8 - nq), (0, 0)))
    ksbn_g = pad_keys(ps['ksb'])
    vsbn_g = pad_keys(ps['vsb'])
    q_rows = jnp.tile(ps['qmla'].reshape(nb, nq, MLA_HEADS * LANES), (1, MLA_HEADS, 1))
    slab_of_lane = jnp.arange(MLA_HEADS * LANES)[None, :] // LANES
    head_of_row = jnp.arange(MLA_HEADS * nq)[:, None] // nq
    qbd = jnp.where((slab_of_lane == head_of_row)[None], q_rows, jnp.zeros((), BF16))
    kmn = jnp.pad(ps['kmla'].reshape(nb, nq, -1), ((0, 0), (0, 16 - nq), (0, 0)))
    vmn = jnp.pad(ps['vmla'].reshape(nb, nq, -1), ((0, 0), (0, 16 - nq), (0, 0)))
    sbo_g, mlao_full = _sample_attention(page_table, kT, vT, cache_mla_ckv, krT, qsb_g, ksbn_g, vsbn_g,
                                         qbd, kmn, vmn, w)
    sbo_s = sbo_g.reshape(nb, SB_KV_HEADS, 2, nq, SB_HEAD_DIM).transpose(0, 3, 1, 2, 4).reshape(
        nb * nq, SB_WIDTH)
    mf = mlao_full.reshape(nb, MLA_HEADS, nq, MLA_HEADS, MLA_V)
    mlao_s = jnp.stack([mf[:, h, :, h, :] for h in range(MLA_HEADS)], axis=2).reshape(nb * nq, MLA_WIDTH)

    def channel_mix(sbo, mlao, x2d, tm):
        h, t, ids, wts = _post_attention(sbo, mlao, x2d, w, tm)
        y = _moe(t, ids[:, :2], wts[:, :2], wg, wu, wd)
        return _combine(h, y, tm)

    y_prompt = channel_mix(sbo_p, mlao_p, xp2, ROW_TILE).reshape(batch, seq, D_MODEL)
    y_sample = channel_mix(sbo_s, mlao_s, xs2, tm_s).reshape(nb, nq, D_MODEL)

    def with_meta(m, p, tail):
        width = int(np.prod(tail))
        mb = jnp.broadcast_to(m[None, :, :width], (batch, N_META, width))
        full = jnp.concatenate([mb, p[:, :width].reshape(batch, seq, width)], axis=1)
        return full.reshape((1, batch, seq + N_META) + tail)

    kv_tail = (SB_KV_HEADS, SB_HEAD_DIM)
    return (y_prompt, y_sample,
            with_meta(pm['ksb'], pp['ksb'], kv_tail), with_meta(pm['vsb'], pp['vsb'], kv_tail),
            with_meta(pm['ckv'], pp['ckv'], (MLA_KV_LORA,)), with_meta(pm['kr'], pp['kr'], (MLA_ROPE,)),
            ps['ksb'].reshape((1, nb, nq) + kv_tail), ps['vsb'].reshape((1, nb, nq) + kv_tail),
            ps['ckv'].reshape(1, nb, nq, MLA_KV_LORA), ps['kr'][:, :MLA_ROPE].reshape(1, nb, nq, MLA_ROPE))
```

```python
import functools

import numpy as np
import jax
import jax.numpy as jnp
from jax import lax
from jax.experimental import pallas as pl
from jax.experimental.pallas import tpu as pltpu

F32 = jnp.float32
BF16 = jnp.bfloat16

D_MODEL = 1024
N_META = 16
EPS = 1e-6
SB_HEADS = 8
SB_KV_HEADS = 4
SB_HEAD_DIM = 64
SB_WIDTH = SB_HEADS * SB_HEAD_DIM
SB_SCALE = SB_HEAD_DIM ** -0.5
MLA_HEADS = 8
MLA_Q_LORA = 256
MLA_KV_LORA = 128
MLA_NOPE = 64
MLA_ROPE = 32
MLA_V = 64
MLA_WIDTH = MLA_HEADS * MLA_V
MLA_SCALE = (MLA_NOPE + MLA_ROPE) ** -0.5
ROPE_THETA = 10000.0
N_GROUPS = 4
EXPERTS_PER_GROUP = 8
N_EXPERTS = N_GROUPS * EXPERTS_PER_GROUP
D_EXPERT = 256

LANES = 128
BLK = 128
MLA_BLK = 256
PAGE = 128
CHUNK_PAGES = 32
SUB_PAGES = 8
ROW_TILE = 256
EXPERT_TILE = 256
VMEM_LIMIT = 56 * 1024 * 1024
SB_DEAD_LOG = -104.0
NEG_BIG = -1e30

_IN_COLS = 512 + 256 + 256 + 256 + 128 + 128 + 128


def _full(shape):
    nd = len(shape)
    return pl.BlockSpec(shape, lambda *_: (0,) * nd)


def _split2(x):
    hi = x.astype(BF16)
    lo = (x - hi.astype(F32)).astype(BF16)
    return hi, lo


def _split3(x):
    hi = x.astype(BF16)
    r = x - hi.astype(F32)
    mid = r.astype(BF16)
    lo = (r - mid.astype(F32)).astype(BF16)
    return hi, mid, lo


def _dot(a, b):
    return jnp.dot(a, b, preferred_element_type=F32)


def _dot_nt(a, b):
    return lax.dot_general(a, b, (((1,), (1,)), ((), ())), preferred_element_type=F32)


def _softplus(z):
    return jnp.maximum(z, 0.0) + jnp.log1p(jnp.exp(-jnp.abs(z)))


def _proj_body(x_ref, cos_ref, sin_ref, n1g_ref, win_ref, cqg_ref, ckvg_ref, wa_ref, wb_ref,
               sq_ref, sqt_ref, g1_ref, g2_ref, g3_ref, gk1_ref, gk3_ref, wukp_ref, sk_ref,
               skt_ref, wuv_ref, wuv2_ref,
               qsb_ref, ksb_ref, vsb_ref, ksb16_ref, vsb16_ref, ckv_ref, kr_ref, qmla_ref,
               kmla_ref, vmla_ref, vmla2_ref):
    x = x_ref[...]
    xn = x * lax.rsqrt(jnp.mean(x * x, axis=-1, keepdims=True) + EPS) * n1g_ref[...]
    p = _dot(xn.astype(BF16), win_ref[...])
    qsb_ref[...] = p[:, 0:512]
    ksb = p[:, 512:768]
    vsb = p[:, 768:1024]
    ksb_ref[...] = ksb
    vsb_ref[...] = vsb
    ksb16_ref[...] = ksb.astype(BF16)
    vsb16_ref[...] = vsb.astype(BF16)

    cq = p[:, 1024:1280]
    cq = cq * lax.rsqrt(jnp.mean(cq * cq, axis=-1, keepdims=True) + EPS) * cqg_ref[...]
    ckv = p[:, 1280:1408]
    ckv = ckv * lax.rsqrt(jnp.mean(ckv * ckv, axis=-1, keepdims=True) + EPS) * ckvg_ref[...]
    ckv_ref[...] = ckv

    cos = cos_ref[...]
    sin = sin_ref[...]
    s1 = p[:, 1408:1536]
    s2 = p[:, 1536:1664]
    inv_kr = lax.rsqrt(jnp.sum(s1 * s1, axis=-1, keepdims=True) * (1.0 / MLA_ROPE) + EPS)
    kr = (s1 * (cos * gk1_ref[...]) + s2 * (sin * gk3_ref[...])) * inv_kr
    kr_ref[...] = kr

    cqb = cq.astype(BF16)
    a = _dot(cqb, wa_ref[...])
    b = _dot(cqb, wb_ref[...])
    hi, lo = _split2(a * a)
    msq = _dot(hi, sq_ref[...]) + _dot(lo, sq_ref[...])
    ihi, ilo = _split2(lax.rsqrt(msq + EPS))
    invf = _dot(ihi, sqt_ref[...]) + _dot(ilo, sqt_ref[...])

    ckvb = ckv.astype(BF16)
    kraw = _dot(ckvb, wukp_ref[...])
    khi, klo = _split2(kraw * kraw)
    kmsq = _dot(khi, sk_ref[...]) + _dot(klo, sk_ref[...])
    kihi, kilo = _split2(lax.rsqrt(kmsq + EPS))
    kinvf = _dot(kihi, skt_ref[...]) + _dot(kilo, skt_ref[...])
    kr_shift = pltpu.roll(kr, 64, axis=1)

    for h in range(MLA_HEADS):
        sl = slice(h * LANES, (h + 1) * LANES)
        qh = (a[:, sl] * (g1_ref[:, sl] + cos * g2_ref[:, sl])
              + b[:, sl] * (sin * g3_ref[:, sl])) * invf[:, sl]
        qmla_ref[:, sl] = qh.astype(BF16)
        kmla_ref[:, sl] = (kraw[:, sl] * kinvf[:, sl] + kr_shift).astype(BF16)
    vmla_ref[...] = _dot(ckvb, wuv_ref[...]).astype(BF16)
    vmla2_ref[...] = _dot(ckvb, wuv2_ref[...]).astype(BF16)


def _project(x2d, cos_t, sin_t, w, tm):
    t = x2d.shape[0]
    tbl_blocks = cos_t.shape[0] // tm
    row = lambda width: pl.BlockSpec((tm, width), lambda i: (i, 0))
    tbl = pl.BlockSpec((tm, LANES), lambda i: (i % tbl_blocks, 0))
    consts = [w['n1g'], w['win'], w['cqg'], w['ckvg'], w['wa'], w['wb'], w['sq'], w['sqt'],
              w['g1'], w['g2'], w['g3'], w['gk1'], w['gk3'], w['wukp'], w['sk'], w['skt'], w['wuv'], w['wuv2']]
    out_widths = [(512, F32), (256, F32), (256, F32), (256, BF16), (256, BF16), (128, F32),
                  (128, F32), (1024, BF16), (1024, BF16), (512, BF16), (1024, BF16)]
    outs = pl.pallas_call(
        _proj_body,
        grid=(t // tm,),
        in_specs=[row(D_MODEL), tbl, tbl] + [_full(c.shape) for c in consts],
        out_specs=[row(wd) for wd, _ in out_widths],
        out_shape=[jax.ShapeDtypeStruct((t, wd), dt) for wd, dt in out_widths],
        compiler_params=pltpu.CompilerParams(dimension_semantics=("arbitrary",),
                                             vmem_limit_bytes=VMEM_LIMIT),
        name="proj",
    )(x2d, cos_t, sin_t, *consts)
    names = ['qsb', 'ksb', 'vsb', 'ksb16', 'vsb16', 'ckv', 'kr', 'qmla', 'kmla', 'vmla', 'vmla2']
    return dict(zip(names, outs))


def _sb_block(q2, kj, vj, vis, u2, carry_ref, acc_ref, k_is_transposed=False):
    z = _dot(q2, kj) if k_is_transposed else _dot_nt(q2, kj)
    sp = _softplus(z)
    log_rest = -sp
    if vis is not None:
        log_rest = jnp.where(vis, log_rest, 0.0)
    hi, mid, lo = _split3(log_rest)
    cs = _dot(hi, u2) + _dot(mid, u2) + _dot(lo, u2)
    later = cs[:, :LANES]
    total = cs[:, LANES:]
    c = carry_ref[...]
    wgt = jnp.exp(z - sp + later + c)
    if vis is not None:
        wgt = jnp.where(vis, wgt, 0.0)
    wb = wgt.astype(BF16)
    acc_ref[...] += _dot_nt(wb, vj) if k_is_transposed else _dot(wb, vj)
    carry_ref[...] = c + total


def _sb_prompt_body(q_ref, k_ref, v_ref, km_ref, vm_ref, u2_ref, o_ref, carry_ref, acc_ref):
    i = pl.program_id(1)
    lane = lax.broadcasted_iota(jnp.int32, (1, LANES), 1)
    lane_lo = lane < 64
    row_t = lax.broadcasted_iota(jnp.int32, (2 * BLK, 1), 0) % BLK
    vis_diag = lane < row_t
    vis_meta = jnp.broadcast_to(lane < N_META, (2 * BLK, LANES))
    u2 = u2_ref[...]
    for g in range(SB_KV_HEADS):
        pair = slice((g // 2) * LANES, (g // 2 + 1) * LANES)
        slab = q_ref[:, g * LANES:(g + 1) * LANES]
        rolled = pltpu.roll(slab, 64, axis=1)
        if g % 2 == 0:
            q_r0 = jnp.where(lane_lo, slab, 0.0)
            q_r1 = jnp.where(lane_lo, rolled, 0.0)
        else:
            q_r0 = jnp.where(lane_lo, 0.0, rolled)
            q_r1 = jnp.where(lane_lo, 0.0, slab)
        q2 = jnp.concatenate([q_r0, q_r1], axis=0).astype(BF16)
        carry_ref[...] = jnp.zeros_like(carry_ref)
        acc_ref[...] = jnp.zeros_like(acc_ref)

        def run(j, vis):
            rows = pl.ds(pl.multiple_of(j * BLK, BLK), BLK)
            _sb_block(q2, k_ref[rows, pair], v_ref[rows, pair], vis, u2, carry_ref, acc_ref)

        run(i, vis_diag)

        def cond(st):
            j, alive = st
            return jnp.logical_and(j >= 0, alive > SB_DEAD_LOG)

        def body(st):
            j, _ = st
            run(j, None)
            return j - 1, jnp.max(carry_ref[...])

        _, alive = lax.while_loop(cond, body, (i - 1, jnp.max(carry_ref[...])))

        @pl.when(alive > SB_DEAD_LOG)
        def _():
            _sb_block(q2, km_ref[:, pair], vm_ref[:, pair], vis_meta, u2, carry_ref, acc_ref)

        acc = acc_ref[...]
        o_r0 = acc[:BLK]
        o_r1 = acc[BLK:]
        if g % 2 == 0:
            out = jnp.where(lane_lo, o_r0, pltpu.roll(o_r1, 64, axis=1))
        else:
            out = jnp.where(lane_lo, pltpu.roll(o_r0, 64, axis=1), o_r1)
        o_ref[:, g * LANES:(g + 1) * LANES] = out


def _sb_prompt(qsb, ksb16, vsb16, km16, vm16, u2, batch, seq):
    nq = seq // BLK
    kv_w = SB_KV_HEADS * SB_HEAD_DIM
    return pl.pallas_call(
        _sb_prompt_body,
        grid=(batch, nq),
        in_specs=[pl.BlockSpec((BLK, SB_WIDTH), lambda b, i: (b * nq + i, 0)),
                  pl.BlockSpec((seq, kv_w), lambda b, i: (b, 0)),
                  pl.BlockSpec((seq, kv_w), lambda b, i: (b, 0)),
                  _full(km16.shape), _full(vm16.shape), _full(u2.shape)],
        out_specs=pl.BlockSpec((BLK, SB_WIDTH), lambda b, i: (b * nq + i, 0)),
        out_shape=jax.ShapeDtypeStruct((batch * seq, SB_WIDTH), F32),
        scratch_shapes=[pltpu.VMEM((2 * BLK, LANES), F32), pltpu.VMEM((2 * BLK, LANES), F32)],
        compiler_params=pltpu.CompilerParams(dimension_semantics=("arbitrary", "arbitrary"),
                                             vmem_limit_bytes=VMEM_LIMIT),
        name="sb_prompt",
    )(qsb, ksb16, vsb16, km16, vm16, u2)


def _mla_prompt_body(q_ref, k_ref, v2_ref, km_ref, vm2_ref, o_ref, m_ref, l_ref, acc_ref):
    i = pl.program_id(1)
    tq = MLA_BLK
    lane = lax.broadcasted_iota(jnp.int32, (1, LANES), 1)
    lane_lo = lane < 64
    col = lax.broadcasted_iota(jnp.int32, (1, tq), 1)
    row_t = lax.broadcasted_iota(jnp.int32, (tq, 1), 0)
    vis_diag = col <= row_t
    vis_meta = jnp.broadcast_to(lane < N_META, (tq, LANES))
    m_ref[...] = jnp.full_like(m_ref, NEG_BIG)
    l_ref[...] = jnp.zeros_like(l_ref)
    acc_ref[...] = jnp.zeros_like(acc_ref)

    def block(load_k, load_v2, vis):
        for hp in range(MLA_HEADS // 2):
            ps, alphas = [], []
            for par in range(2):
                h = 2 * hp + par
                s = _dot_nt(q_ref[:, h * LANES:(h + 1) * LANES], load_k(h))
                if vis is not None:
                    s = jnp.where(vis, s, NEG_BIG)
                m_old = m_ref[h]
                m_new = jnp.maximum(m_old, jnp.max(s, axis=-1, keepdims=True))
                p = jnp.exp(s - m_new)
                alpha = jnp.exp(m_old - m_new)
                l_ref[h] = alpha * l_ref[h] + jnp.sum(p, axis=-1, keepdims=True)
                m_ref[h] = m_new
                ps.append(p.astype(BF16))
                alphas.append(alpha)
            pcat = jnp.concatenate(ps, axis=1)
            vbd = jnp.concatenate([load_v2(2 * hp), load_v2(2 * hp + 1)], axis=0)
            acc_ref[hp] = acc_ref[hp] * jnp.where(lane_lo, alphas[0], alphas[1]) + _dot(pcat, vbd)

    block(lambda h: km_ref[:, h * LANES:(h + 1) * LANES],
          lambda c: vm2_ref[:, c * LANES:(c + 1) * LANES], vis_meta)

    def rows_of(j):
        return pl.ds(pl.multiple_of(j * tq, tq), tq)

    def body(j, carry):
        rows = rows_of(j)
        block(lambda h: k_ref[rows, h * LANES:(h + 1) * LANES],
              lambda c: v2_ref[rows, c * LANES:(c + 1) * LANES], None)
        return carry

    lax.fori_loop(0, i, body, 0)
    rows = rows_of(i)
    block(lambda h: k_ref[rows, h * LANES:(h + 1) * LANES],
          lambda c: v2_ref[rows, c * LANES:(c + 1) * LANES], vis_diag)
    for hp in range(MLA_HEADS // 2):
        o_ref[:, hp * LANES:(hp + 1) * LANES] = acc_ref[hp] / jnp.where(lane_lo, l_ref[2 * hp], l_ref[2 * hp + 1])


def _mla_prompt(qmla, kmla, vmla2, kmm, vmm2, batch, seq):
    nq = seq // MLA_BLK
    width = MLA_HEADS * LANES
    return pl.pallas_call(
        _mla_prompt_body,
        grid=(batch, nq),
        in_specs=[pl.BlockSpec((MLA_BLK, width), lambda b, i: (b * nq + i, 0)),
                  pl.BlockSpec((seq, width), lambda b, i: (b, 0)),
                  pl.BlockSpec((seq, width), lambda b, i: (b, 0)),
                  _full(kmm.shape), _full(vmm2.shape)],
        out_specs=pl.BlockSpec((MLA_BLK, MLA_WIDTH), lambda b, i: (b * nq + i, 0)),
        out_shape=jax.ShapeDtypeStruct((batch * seq, MLA_WIDTH), F32),
        scratch_shapes=[pltpu.VMEM((MLA_HEADS, MLA_BLK, 1), F32), pltpu.VMEM((MLA_HEADS, MLA_BLK, 1), F32),
                        pltpu.VMEM((MLA_HEADS // 2, MLA_BLK, LANES), F32)],
        compiler_params=pltpu.CompilerParams(dimension_semantics=("arbitrary", "arbitrary"),
                                             vmem_limit_bytes=VMEM_LIMIT),
        name="mla_prompt",
    )(qmla, kmla, vmla2, kmm, vmm2)


def _sample_body(pt_ref,
                 kT_hbm, vT_hbm, ckv_hbm, krT_hbm,
                 qsb_ref, ksbn_ref, vsbn_ref, qbd_ref, kmn_ref, vmn_ref,
                 wukt_ref, wabs_ref, shs_ref, wuv_ref, u2_ref,
                 sbo_ref, mlao_ref,
                 cbuf, rbuf, sbk, sbv, wq_ref, qr_ref, m_ref, l_ref, lat_ref,
                 carry_ref, sbacc_ref, sem_c, sem_r, sem_sb):
    b = pl.program_id(0)
    c = pl.program_id(1)
    nb = pl.num_programs(0)
    nc = pl.num_programs(1)
    n_pages = nc * CHUNK_PAGES
    step = b * nc + c
    slot = step % 2
    chunk = CHUNK_PAGES * PAGE
    rows_q = MLA_HEADS * 4

    def chunk_copies(bb, cc, sl):
        cps = []
        for pg in range(CHUNK_PAGES):
            page = pt_ref[bb, cc * CHUNK_PAGES + pg]
            cps.append(pltpu.make_async_copy(
                ckv_hbm.at[0, page], cbuf.at[sl, pl.ds(pg * PAGE, PAGE), :], sem_c.at[sl]))
            cps.append(pltpu.make_async_copy(
                krT_hbm.at[0, page], rbuf.at[sl, :, pl.ds(pg * PAGE, PAGE)], sem_r.at[sl]))
        return cps

    def sb_copies(bb, page_idx, sl):
        page = pt_ref[bb, page_idx]
        return [pltpu.make_async_copy(kT_hbm.at[0, page], sbk.at[sl], sem_sb.at[sl, 0]),
                pltpu.make_async_copy(vT_hbm.at[0, page], sbv.at[sl], sem_sb.at[sl, 1])]

    @pl.when(step == 0)
    def _():
        for cp in chunk_copies(0, 0, 0):
            cp.start()

    @pl.when(c == 0)
    def _():
        for sl in range(2):
            for cp in sb_copies(b, n_pages - 1 - sl, sl):
                cp.start()
        qa = _dot(qbd_ref[0], wabs_ref[...])
        wq_ref[0:512, :] = wukt_ref[...]
        wq_ref[512:512 + rows_q, :] = qa[:, :LANES].astype(BF16)
        qr_ref[...] = qa[:, LANES:LANES + MLA_ROPE].astype(BF16)
        m_ref[...] = jnp.full_like(m_ref, NEG_BIG)
        l_ref[...] = jnp.zeros_like(l_ref)
        lat_ref[...] = jnp.zeros_like(lat_ref)

    for cp in chunk_copies(b, c, slot):
        cp.wait()

    @pl.when(step + 1 < nb * nc)
    def _():
        nxt = step + 1
        for cp in chunk_copies(nxt // nc, nxt % nc, 1 - slot):
            cp.start()

    sub = SUB_PAGES * PAGE
    cbs, scores = [], []
    for u in range(CHUNK_PAGES // SUB_PAGES):
        cols = slice(u * sub, (u + 1) * sub)
        cb = cbuf[slot, cols, :].astype(BF16)
        big = _dot_nt(wq_ref[...], cb)
        k2 = big[0:512] * big[0:512]
        part = k2.reshape(MLA_HEADS, MLA_NOPE // 8, 8, sub).sum(axis=1).reshape(MLA_HEADS * 8, sub)
        ksum = _dot(shs_ref[...], part.astype(BF16))
        scores.append(big[512:512 + rows_q] * lax.rsqrt(ksum + EPS)
                      + _dot(qr_ref[...], rbuf[slot, :, cols].astype(BF16)))
        cbs.append(cb)
    s = jnp.concatenate(scores, axis=1)
    m_old = m_ref[...]
    m_new = jnp.maximum(m_old, jnp.max(s, axis=-1, keepdims=True))
    p = jnp.exp(s - m_new).astype(BF16)
    alpha = jnp.exp(m_old - m_new)
    l_ref[...] = alpha * l_ref[...] + jnp.sum(p.astype(F32), axis=-1, keepdims=True)
    pv = _dot(p[:, 0:sub], cbs[0])
    for u in range(1, len(cbs)):
        pv += _dot(p[:, u * sub:(u + 1) * sub], cbs[u])
    lat_ref[...] = alpha * lat_ref[...] + pv
    m_ref[...] = m_new

    @pl.when(c == nc - 1)
    def _():
        lane8 = lax.broadcasted_iota(jnp.int32, (rows_q, 16), 1)
        q_of_row = lax.broadcasted_iota(jnp.int32, (rows_q, 16), 0) % 4
        s_new = _dot_nt(qbd_ref[0], kmn_ref[0])
        s_new = jnp.where(lane8 <= q_of_row, s_new, NEG_BIG)
        m_old = m_ref[...]
        m_fin = jnp.maximum(m_old, jnp.max(s_new, axis=-1, keepdims=True))
        p_new = jnp.exp(s_new - m_fin)
        alpha = jnp.exp(m_old - m_fin)
        l_fin = alpha * l_ref[...] + jnp.sum(p_new, axis=-1, keepdims=True)
        o = _dot((alpha * lat_ref[...]).astype(BF16), wuv_ref[...]) + _dot(p_new.astype(BF16), vmn_ref[0])
        mlao_ref[0] = o / l_fin

        u2 = u2_ref[...]
        lane = lax.broadcasted_iota(jnp.int32, (1, LANES), 1)
        q_row = lax.broadcasted_iota(jnp.int32, (8, 1), 0) % 4
        vis_new = lane < q_row
        zpad = jnp.zeros((PAGE - 8, SB_HEAD_DIM), F32)
        for g in range(SB_KV_HEADS):
            qg = qsb_ref[0, g].astype(BF16)
            rows = slice(g * 8, (g + 1) * 8)
            carry = carry_ref.at[rows]
            acc = sbacc_ref.at[rows]
            carry[...] = jnp.zeros((8, LANES), F32)
            acc[...] = jnp.zeros((8, SB_HEAD_DIM), F32)
            kn = jnp.concatenate([ksbn_ref[0, g], zpad], axis=0).astype(BF16)
            vn = jnp.concatenate([vsbn_ref[0, g], zpad], axis=0).astype(BF16)
            _sb_block(qg, kn, vn, vis_new, u2, carry, acc)

        def sb_page(sl):
            for g in range(SB_KV_HEADS):
                qg = qsb_ref[0, g].astype(BF16)
                rows = slice(g * 8, (g + 1) * 8)
                _sb_block(qg, sbk[sl, g].astype(BF16), sbv[sl, g].astype(BF16), None, u2,
                          carry_ref.at[rows], sbacc_ref.at[rows], k_is_transposed=True)

        for sl in range(2):
            for cp in sb_copies(b, n_pages - 1 - sl, sl):
                cp.wait()
            sb_page(sl)

        def cond(st):
            j, alive = st
            return jnp.logical_and(j >= 0, alive > SB_DEAD_LOG)

        def body(st):
            j, _ = st
            cps = sb_copies(b, j, 0)
            for cp in cps:
                cp.start()
            for cp in cps:
                cp.wait()
            sb_page(0)
            return j - 1, jnp.max(carry_ref[...])

        lax.while_loop(cond, body, (n_pages - 3, jnp.max(carry_ref[...])))
        for g in range(SB_KV_HEADS):
            sbo_ref[0, g] = sbacc_ref[g * 8:(g + 1) * 8, :]


def _sample_attention(page_table, kT, vT, ckv_cache, krT, qsb_g, ksbn_g, vsbn_g, qbd, kmn, vmn, w):
    nb, n_pages = page_table.shape
    nc = n_pages // CHUNK_PAGES
    chunk = CHUNK_PAGES * PAGE
    rows_q = MLA_HEADS * 4
    per_b = lambda shape: pl.BlockSpec((1,) + shape, lambda b, c, pt: (b,) + (0,) * len(shape))
    const = lambda a: pl.BlockSpec(a.shape, lambda b, c, pt: (0,) * a.ndim)
    consts = [w['wukt'], w['wabs'], w['shs'], w['wuv'], w['u2']]
    any_spec = pl.BlockSpec(memory_space=pl.ANY)
    return pl.pallas_call(
        _sample_body,
        grid_spec=pltpu.PrefetchScalarGridSpec(
            num_scalar_prefetch=1,
            grid=(nb, nc),
            in_specs=[any_spec, any_spec, any_spec, any_spec,
                      per_b((SB_KV_HEADS, 8, SB_HEAD_DIM)), per_b((SB_KV_HEADS, 8, SB_HEAD_DIM)),
                      per_b((SB_KV_HEADS, 8, SB_HEAD_DIM)), per_b((rows_q, MLA_HEADS * LANES)),
                      per_b((16, MLA_HEADS * LANES)), per_b((16, MLA_WIDTH))]
                     + [const(a) for a in consts],
            out_specs=[per_b((SB_KV_HEADS, 8, SB_HEAD_DIM)), per_b((rows_q, MLA_WIDTH))],
            scratch_shapes=[
                pltpu.VMEM((2, chunk, MLA_KV_LORA), F32),
                pltpu.VMEM((2, MLA_ROPE, chunk), F32),
                pltpu.VMEM((2, SB_KV_HEADS, SB_HEAD_DIM, PAGE), F32),
                pltpu.VMEM((2, SB_KV_HEADS, SB_HEAD_DIM, PAGE), F32),
                pltpu.VMEM((512 + rows_q, MLA_KV_LORA), BF16),
                pltpu.VMEM((rows_q, MLA_ROPE), BF16),
                pltpu.VMEM((rows_q, 1), F32),
                pltpu.VMEM((rows_q, 1), F32),
                pltpu.VMEM((rows_q, MLA_KV_LORA), F32),
                pltpu.VMEM((SB_KV_HEADS * 8, LANES), F32),
                pltpu.VMEM((SB_KV_HEADS * 8, SB_HEAD_DIM), F32),
                pltpu.SemaphoreType.DMA((2,)),
                pltpu.SemaphoreType.DMA((2,)),
                pltpu.SemaphoreType.DMA((2, 2)),
            ]),
        out_shape=[jax.ShapeDtypeStruct((nb, SB_KV_HEADS, 8, SB_HEAD_DIM), F32),
                   jax.ShapeDtypeStruct((nb, rows_q, MLA_WIDTH), F32)],
        compiler_params=pltpu.CompilerParams(dimension_semantics=("arbitrary", "arbitrary"),
                                             vmem_limit_bytes=VMEM_LIMIT),
        name="sample_attn",
    )(page_table, kT, vT, ckv_cache, krT, qsb_g, ksbn_g, vsbn_g, qbd, kmn, vmn, *consts)


def _post_body(sbo_ref, mlao_ref, x_ref, gsb_ref, gmla_ref, wo_ref, n2g_ref, wrh_ref, wrl_ref,
               br_ref, ltri_ref, h_ref, t_ref, ids_ref, wts_ref, counts_ref, cnt_ref):
    sbo = sbo_ref[...]
    mlao = mlao_ref[...]
    m_sb = sbo * lax.rsqrt(jnp.mean(sbo * sbo, axis=-1, keepdims=True) + EPS) * gsb_ref[...]
    m_mla = mlao * lax.rsqrt(jnp.mean(mlao * mlao, axis=-1, keepdims=True) + EPS) * gmla_ref[...]
    h = (x_ref[...] + _dot(m_sb.astype(BF16), wo_ref[0:SB_WIDTH, :])
         + _dot(m_mla.astype(BF16), wo_ref[SB_WIDTH:, :]))
    h_ref[...] = h
    t = h * lax.rsqrt(jnp.mean(h * h, axis=-1, keepdims=True) + EPS) * n2g_ref[...]
    t_ref[...] = t
    thi, tlo = _split2(t)
    lg = (_dot(thi, wrh_ref[...]) + _dot(tlo, wrh_ref[...]) + _dot(thi, wrl_ref[...])
          + br_ref[...])
    tm = lg.shape[0]
    lane_i = lax.broadcasted_iota(jnp.int32, (tm, LANES), 1)
    lane = lane_i.astype(F32)
    big_i = jnp.float32(1 << 20)
    is_grp = lane < N_GROUPS
    mg = jnp.max(jnp.where(is_grp, lg, -jnp.inf), axis=-1, keepdims=True)
    gidx = jnp.min(jnp.where(is_grp & (lg == mg), lane, big_i), axis=-1, keepdims=True)
    p_grp = 1.0 / jnp.sum(jnp.where(is_grp, jnp.exp(lg - mg), 0.0), axis=-1, keepdims=True)
    lo_lane = N_GROUPS + gidx * EXPERTS_PER_GROUP
    sel = (lane >= lo_lane) & (lane < lo_lane + EXPERTS_PER_GROUP)
    v1 = jnp.max(jnp.where(sel, lg, -jnp.inf), axis=-1, keepdims=True)
    i1 = jnp.min(jnp.where(sel & (lg == v1), lane, big_i), axis=-1, keepdims=True)
    sel2 = sel & (lane != i1)
    v2 = jnp.max(jnp.where(sel2, lg, -jnp.inf), axis=-1, keepdims=True)
    i2 = jnp.min(jnp.where(sel2 & (lg == v2), lane, big_i), axis=-1, keepdims=True)
    e21 = jnp.exp(v2 - v1)
    w1 = p_grp / (1.0 + e21)
    w2 = w1 * e21
    e1 = i1 - N_GROUPS
    e2 = i2 - N_GROUPS
    @pl.when(pl.program_id(0) == 0)
    def _():
        cnt_ref[...] = jnp.zeros_like(cnt_ref)

    oh1 = jnp.where(lane == e1, 1.0, 0.0)
    oh2 = jnp.where(lane == e2, 1.0, 0.0)
    both = oh1 + oh2
    before = _dot(ltri_ref[...], both.astype(BF16)) + cnt_ref[0:1, :]
    r1 = jnp.sum(oh1 * before, axis=-1, keepdims=True)
    r2 = jnp.sum(oh2 * before, axis=-1, keepdims=True)
    cnt_new = cnt_ref[...] + jnp.sum(both, axis=0, keepdims=True)
    cnt_ref[...] = cnt_new
    counts_ref[...] = cnt_new
    ids = jnp.where(lane_i == 0, e1, jnp.where(lane_i == 1, e2,
                    jnp.where(lane_i == 2, r1, jnp.where(lane_i == 3, r2, 0.0))))
    ids_ref[...] = ids.astype(jnp.int32)
    wts_ref[...] = jnp.where(lane_i == 0, w1, jnp.where(lane_i == 1, w2, 0.0))


def _post_attention(sbo, mlao, x2d, w, tm):
    t = x2d.shape[0]
    row = lambda width: pl.BlockSpec((tm, width), lambda i: (i, 0))
    ltri = jnp.asarray(np.tril(np.ones((tm, tm), np.float32), -1), dtype=BF16)
    consts = [w['gsb'], w['gmla'], w['wo'], w['n2g'], w['wrh'], w['wrl'], w['br'], ltri]
    return pl.pallas_call(
        _post_body,
        grid=(t // tm,),
        in_specs=[row(SB_WIDTH), row(MLA_WIDTH), row(D_MODEL)] + [_full(c.shape) for c in consts],
        out_specs=[row(D_MODEL), row(D_MODEL), row(LANES), row(LANES), _full((8, LANES))],
        out_shape=[jax.ShapeDtypeStruct((t, D_MODEL), F32), jax.ShapeDtypeStruct((t, D_MODEL), F32),
                   jax.ShapeDtypeStruct((t, LANES), jnp.int32), jax.ShapeDtypeStruct((t, LANES), F32),
                   jax.ShapeDtypeStruct((8, LANES), F32)],
        scratch_shapes=[pltpu.VMEM((8, LANES), F32)],
        compiler_params=pltpu.CompilerParams(dimension_semantics=("arbitrary",),
                                             vmem_limit_bytes=VMEM_LIMIT),
        name="post_attn",
    )(sbo, mlao, x2d, *consts)


def _wait_rows(src, dst, sem, cnt):
    def wait8(_, c):
        pltpu.make_async_copy(src.at[pl.ds(0, 8), :], dst.at[pl.ds(0, 8), :], sem).wait()
        return c

    def wait1(_, c):
        pltpu.make_async_copy(src.at[pl.ds(0, 1), :], dst.at[pl.ds(0, 1), :], sem).wait()
        return c

    lax.fori_loop(0, cnt // 8, wait8, 0)
    lax.fori_loop(0, cnt % 8, wait1, 0)


def _dispatch_body(pos_ref, tile_cnt_ref, npad_ref, t_ref, xs_hbm, zbuf, sem, sem_pad, sem_tile):
    i = pl.program_id(0)
    tm = t_ref.shape[0]

    def empty_tile_copy(n):
        return pltpu.make_async_copy(zbuf, xs_hbm.at[pl.ds(n * EXPERT_TILE, EXPERT_TILE), :], sem_tile)

    @pl.when(i == 0)
    def _():
        zbuf[...] = jnp.zeros_like(zbuf)

        def per_tile(n, c):
            cnt = tile_cnt_ref[n]

            @pl.when(cnt == 0)
            def _():
                empty_tile_copy(n).start()

            def per_row(r, c2):
                pltpu.make_async_copy(zbuf.at[pl.ds(0, 1), :],
                                      xs_hbm.at[pl.ds(n * EXPERT_TILE + r, 1), :], sem_pad).start()
                return c2

            lax.fori_loop(jnp.where(cnt > 0, cnt, EXPERT_TILE), EXPERT_TILE, per_row, 0)
            return c

        lax.fori_loop(0, tile_cnt_ref.shape[0], per_tile, 0)

    def send(r, c):
        a = 2 * (i * tm + r)
        for k in range(2):
            pltpu.make_async_copy(t_ref.at[pl.ds(r, 1), :],
                                  xs_hbm.at[pl.ds(pos_ref[a + k], 1), :], sem).start()
        return c

    lax.fori_loop(0, tm, send, 0, unroll=4)
    for _ in range(2):
        pltpu.make_async_copy(t_ref, xs_hbm.at[pl.ds(0, tm), :], sem).wait()

    @pl.when(i == pl.num_programs(0) - 1)
    def _():
        _wait_rows(zbuf, xs_hbm, sem_pad, npad_ref[0])

        def wait_tile(_, c):
            empty_tile_copy(0).wait()
            return c

        lax.fori_loop(0, npad_ref[1], wait_tile, 0)


def _dispatch(t2d, pos_flat, tile_cnt, n_pad, n_rows, tm):
    t = t2d.shape[0]
    return pl.pallas_call(
        _dispatch_body,
        grid_spec=pltpu.PrefetchScalarGridSpec(
            num_scalar_prefetch=3,
            grid=(t // tm,),
            in_specs=[pl.BlockSpec((tm, D_MODEL), lambda i, *_: (i, 0))],
            out_specs=pl.BlockSpec(memory_space=pl.ANY),
            scratch_shapes=[pltpu.VMEM((EXPERT_TILE, D_MODEL), F32), pltpu.SemaphoreType.DMA(()),
                            pltpu.SemaphoreType.DMA(()), pltpu.SemaphoreType.DMA(())]),
        out_shape=jax.ShapeDtypeStruct((n_rows, D_MODEL), F32),
        compiler_params=pltpu.CompilerParams(dimension_semantics=("arbitrary",),
                                             vmem_limit_bytes=VMEM_LIMIT),
        name="moe_dispatch",
    )(pos_flat, tile_cnt, n_pad, t2d)


def _moe_body(tile_e_ref, tile_src_ref, tile_cnt_ref, x_ref, wg_ref, wu_ref, wd_ref, y_ref):
    cnt = tile_cnt_ref[pl.program_id(0)]

    @pl.when(cnt == 0)
    def _():
        y_ref[...] = jnp.zeros_like(y_ref)

    @pl.when(cnt > 0)
    def _():
        xb = x_ref[...].astype(BF16)
        gate = _dot(xb, wg_ref[0].astype(BF16))
        up = _dot(xb, wu_ref[0].astype(BF16))
        hid = gate * jax.nn.sigmoid(gate) * up
        y_ref[...] = _dot(hid.astype(BF16), wd_ref[0].astype(BF16))


def _experts(xs, tile_e, tile_src, tile_cnt, w_gate, w_up, w_down):
    n_tiles = tile_e.shape[0]
    wspec = lambda shape: pl.BlockSpec((1,) + shape, lambda n, te, ts, tc: (te[n], 0, 0))
    rows = pl.BlockSpec((EXPERT_TILE, D_MODEL), lambda n, te, ts, tc: (ts[n], 0))
    return pl.pallas_call(
        _moe_body,
        grid_spec=pltpu.PrefetchScalarGridSpec(
            num_scalar_prefetch=3,
            grid=(n_tiles,),
            in_specs=[rows, wspec((D_MODEL, D_EXPERT)), wspec((D_MODEL, D_EXPERT)),
                      wspec((D_EXPERT, D_MODEL))],
            out_specs=pl.BlockSpec((EXPERT_TILE, D_MODEL), lambda n, te, ts, tc: (n, 0))),
        out_shape=jax.ShapeDtypeStruct(xs.shape, F32),
        compiler_params=pltpu.CompilerParams(dimension_semantics=("arbitrary",),
                                             vmem_limit_bytes=VMEM_LIMIT),
        name="moe",
    )(tile_e, tile_src, tile_cnt, xs, w_gate, w_up, w_down)


def _combine_body(pos_ref, h_ref, wts_ref, ys_hbm, o_ref, y0buf, y1buf, sem):
    i = pl.program_id(0)
    tm = h_ref.shape[0]

    def fetch(r, c):
        a = 2 * (i * tm + r)
        pltpu.make_async_copy(ys_hbm.at[pl.ds(pos_ref[a], 1), :], y0buf.at[pl.ds(r, 1), :], sem).start()
        pltpu.make_async_copy(ys_hbm.at[pl.ds(pos_ref[a + 1], 1), :], y1buf.at[pl.ds(r, 1), :], sem).start()
        return c

    lax.fori_loop(0, tm, fetch, 0, unroll=4)
    pltpu.make_async_copy(ys_hbm.at[pl.ds(0, tm), :], y0buf, sem).wait()
    pltpu.make_async_copy(ys_hbm.at[pl.ds(0, tm), :], y1buf, sem).wait()
    wts = wts_ref[...]
    o_ref[...] = h_ref[...] + wts[:, 0:1] * y0buf[...] + wts[:, 1:2] * y1buf[...]


def _combine(h2d, wts, ys, pos_flat, tm):
    t = h2d.shape[0]
    return pl.pallas_call(
        _combine_body,
        grid_spec=pltpu.PrefetchScalarGridSpec(
            num_scalar_prefetch=1,
            grid=(t // tm,),
            in_specs=[pl.BlockSpec((tm, D_MODEL), lambda i, p: (i, 0)),
                      pl.BlockSpec((tm, LANES), lambda i, p: (i, 0)),
                      pl.BlockSpec(memory_space=pl.ANY)],
            out_specs=pl.BlockSpec((tm, D_MODEL), lambda i, p: (i, 0)),
            scratch_shapes=[pltpu.VMEM((tm, D_MODEL), F32), pltpu.VMEM((tm, D_MODEL), F32),
                            pltpu.SemaphoreType.DMA(())]),
        out_shape=jax.ShapeDtypeStruct((t, D_MODEL), F32),
        compiler_params=pltpu.CompilerParams(dimension_semantics=("arbitrary",),
                                             vmem_limit_bytes=VMEM_LIMIT),
        name="moe_combine",
    )(pos_flat, h2d, wts, ys)


def _moe(t2d, h2d, ids, wts, counts_f, w_gate, w_up, w_down, tm):
    t = t2d.shape[0]
    n_tiles = (2 * t) // EXPERT_TILE + N_EXPERTS
    counts = counts_f[0, :N_EXPERTS].astype(jnp.int32)
    tiles_e = (counts + EXPERT_TILE - 1) // EXPERT_TILE
    tile_end = jnp.cumsum(tiles_e)
    tile_start = tile_end - tiles_e
    total = tile_end[-1]
    tile_ids = jnp.arange(n_tiles, dtype=jnp.int32)
    tile_e = jnp.minimum(jnp.sum((tile_ids[:, None] >= tile_end[None, :]).astype(jnp.int32), axis=1),
                         N_EXPERTS - 1)
    onehot_t = (tile_e[:, None] == jnp.arange(N_EXPERTS)[None, :]).astype(jnp.int32)
    left = jnp.sum(onehot_t * (counts[None, :] - (tile_ids[:, None] - tile_start[None, :]) * EXPERT_TILE), axis=1)
    tile_cnt = jnp.where(tile_ids < total, jnp.clip(left, 0, EXPERT_TILE), 0).astype(jnp.int32)
    tile_src = jnp.minimum(tile_ids, total - 1).astype(jnp.int32)
    n_pad = jnp.stack([jnp.sum(jnp.where(tile_cnt > 0, EXPERT_TILE - tile_cnt, 0)),
                       jnp.sum((tile_cnt == 0).astype(jnp.int32))]).astype(jnp.int32)
    experts = ids[:, 0:2]
    onehot_a = (experts[:, :, None] == jnp.arange(N_EXPERTS)[None, None, :]).astype(jnp.int32)
    pos = jnp.sum(onehot_a * tile_start[None, None, :], axis=-1) * EXPERT_TILE + ids[:, 2:4]
    pos_flat = pos.reshape(-1).astype(jnp.int32)
    xs = _dispatch(t2d, pos_flat, tile_cnt, n_pad, n_tiles * EXPERT_TILE, tm)
    ys = _experts(xs, tile_e.astype(jnp.int32), tile_src, tile_cnt, w_gate, w_up, w_down)
    return _combine(h2d, wts, ys, pos_flat, tm)


def _rot_half_cols(wr):
    half = MLA_ROPE // 2
    return jnp.concatenate([-wr[..., half:], wr[..., :half]], axis=-1)


def _swap_halves(g):
    half = MLA_ROPE // 2
    return jnp.concatenate([g[..., half:], g[..., :half]], axis=-1)


def _group_sum_consts():
    sq = np.zeros((MLA_HEADS * LANES, LANES), np.float32)
    sqt = np.zeros((LANES, MLA_HEADS * LANES), np.float32)
    sk = np.zeros((MLA_HEADS * LANES, LANES), np.float32)
    skt = np.zeros((LANES, MLA_HEADS * LANES), np.float32)
    for h in range(MLA_HEADS):
        nope = slice(h * LANES, h * LANES + MLA_NOPE)
        rope = slice(h * LANES + MLA_NOPE, h * LANES + MLA_NOPE + MLA_ROPE)
        sq[nope, 2 * h] = 1.0 / MLA_NOPE
        sq[rope, 2 * h + 1] = 1.0 / MLA_ROPE
        sqt[2 * h, nope] = 1.0
        sqt[2 * h + 1, rope] = 1.0
        sk[nope, h] = 1.0 / MLA_NOPE
        skt[h, nope] = 1.0
    u2 = np.zeros((LANES, 2 * LANES), np.float32)
    u2[:, :LANES] = (np.arange(LANES)[:, None] > np.arange(LANES)[None, :]).astype(np.float32)
    u2[:, LANES:] = 1.0
    shs = np.zeros((MLA_HEADS * 4, MLA_HEADS * 8), np.float32)
    for h in range(MLA_HEADS):
        shs[h * 4:(h + 1) * 4, h * 8:(h + 1) * 8] = 1.0 / MLA_NOPE
    as16 = lambda a: jnp.asarray(a, dtype=BF16)
    return dict(sq=as16(sq), sqt=as16(sqt), sk=as16(sk), skt=as16(skt), u2=as16(u2), shs=as16(shs))


def _prep_weights(norm1_g, w_in, cq_norm_g, ckv_norm_g, w_uq, qn_norm_g, qr_norm_g, kr_norm_g,
                  w_uk, kn_norm_g, w_uv, sb_out_norm_g, mla_out_norm_g, w_o, norm2_g,
                  w_router_group, b_router_group, w_router_expert, b_router_expert):
    w = _group_sum_consts()
    wr = w_in[:, 1408:1440]
    pad96 = jnp.zeros((D_MODEL, LANES - MLA_ROPE), F32)
    w['win'] = jnp.concatenate([w_in[:, :512] * SB_SCALE, w_in[:, 512:1408], wr, pad96,
                                _rot_half_cols(wr), pad96], axis=1).astype(BF16)
    w['n1g'] = norm1_g[None, :]
    w['cqg'] = cq_norm_g[None, :]
    w['ckvg'] = ckv_norm_g[None, :]
    w['wa'] = jnp.pad(w_uq, ((0, 0), (0, 0), (0, LANES - MLA_NOPE - MLA_ROPE))).reshape(
        MLA_Q_LORA, MLA_HEADS * LANES).astype(BF16)
    w['wb'] = jnp.pad(_rot_half_cols(w_uq[:, :, MLA_NOPE:]),
                      ((0, 0), (0, 0), (MLA_NOPE, LANES - MLA_NOPE - MLA_ROPE))).reshape(
        MLA_Q_LORA, MLA_HEADS * LANES).astype(BF16)
    z32 = jnp.zeros((LANES - MLA_NOPE - MLA_ROPE,), F32)
    z64 = jnp.zeros((MLA_NOPE,), F32)
    z96 = jnp.zeros((LANES - MLA_ROPE,), F32)
    head_row = lambda v: jnp.tile(v, MLA_HEADS)[None, :]
    w['g1'] = head_row(jnp.concatenate([qn_norm_g * kn_norm_g * MLA_SCALE, jnp.zeros((64,), F32)]))
    w['g2'] = head_row(jnp.concatenate([z64, qr_norm_g * MLA_SCALE, z32]))
    w['g3'] = head_row(jnp.concatenate([z64, _swap_halves(qr_norm_g) * MLA_SCALE, z32]))
    w['gk1'] = jnp.concatenate([kr_norm_g, z96])[None, :]
    w['gk3'] = jnp.concatenate([_swap_halves(kr_norm_g), z96])[None, :]
    w['wukp'] = jnp.pad(w_uk, ((0, 0), (0, 0), (0, LANES - MLA_NOPE))).reshape(
        MLA_KV_LORA, MLA_HEADS * LANES).astype(BF16)
    w['wuv'] = w_uv.reshape(MLA_KV_LORA, MLA_WIDTH).astype(BF16)
    pairs = w_uv.reshape(MLA_KV_LORA, MLA_HEADS // 2, 2, MLA_V)
    zero_v = jnp.zeros_like(pairs[:, :, 0])
    keep_even = jnp.stack([pairs[:, :, 0], zero_v], axis=2)
    keep_odd = jnp.stack([zero_v, pairs[:, :, 1]], axis=2)
    w['wuv2'] = jnp.stack([keep_even, keep_odd], axis=2).reshape(MLA_KV_LORA, 2 * MLA_WIDTH).astype(BF16)
    w['wukt'] = w_uk.reshape(MLA_KV_LORA, MLA_HEADS * MLA_NOPE).T.astype(BF16)
    wabs_nope = jnp.pad(jnp.transpose(w_uk, (1, 2, 0)), ((0, 0), (0, LANES - MLA_NOPE), (0, 0)))
    e_r = np.zeros((MLA_HEADS, LANES, LANES), np.float32)
    for j in range(MLA_ROPE):
        e_r[:, MLA_NOPE + j, j] = 1.0
    w['wabs'] = jnp.concatenate([wabs_nope, jnp.asarray(e_r)], axis=-1).reshape(
        MLA_HEADS * LANES, 2 * LANES).astype(BF16)
    w['gsb'] = sb_out_norm_g[None, :]
    w['gmla'] = mla_out_norm_g[None, :]
    w['wo'] = w_o.astype(BF16)
    w['n2g'] = norm2_g[None, :]
    wr_all = jnp.pad(jnp.concatenate([w_router_group, w_router_expert], axis=1),
                     ((0, 0), (0, LANES - N_GROUPS - N_EXPERTS)))
    w['wrh'], w['wrl'] = _split2(wr_all)
    w['br'] = jnp.pad(jnp.concatenate([b_router_group, b_router_expert]),
                      (0, LANES - N_GROUPS - N_EXPERTS))[None, :]
    return w


def _rope_tables(pos):
    half = MLA_ROPE // 2
    inv_freq = ROPE_THETA ** (-jnp.arange(half, dtype=F32) / half)
    ang = pos.astype(F32)[:, None] * inv_freq[None, :]
    z = jnp.zeros((pos.shape[0], 32), F32)

    def slab(v):
        return jnp.concatenate([v, v, z, v, v, z], axis=1)

    return slab(jnp.cos(ang)), slab(jnp.sin(ang))


def _pad_rows(a, rows):
    return jnp.pad(a, ((0, rows - a.shape[0]), (0, 0)))


def _kv_out(a, batch, length, tail):
    return a.reshape((1, batch, length) + tail)


def kernel(x_prompt, x_sample, cache_sb_k, cache_sb_v, cache_mla_ckv, cache_mla_krope, page_table, meta_tokens, norm1_g, w_in, cq_norm_g, ckv_norm_g, w_uq, qn_norm_g, qr_norm_g, kr_norm_g, w_uk, kn_norm_g, w_uv, sb_out_norm_g, mla_out_norm_g, w_o, norm2_g, w_router_group, b_router_group, w_router_expert, b_router_expert, w_gate, w_up, w_down):
    batch, seq, _ = x_prompt.shape
    nb, nq, _ = x_sample.shape
    n_past = page_table.shape[1] * cache_sb_k.shape[2]
    assert nq == 4 and seq % MLA_BLK == 0 and page_table.shape[1] % CHUNK_PAGES == 0
    w = _prep_weights(norm1_g[0], w_in[0], cq_norm_g[0], ckv_norm_g[0], w_uq[0], qn_norm_g[0],
                      qr_norm_g[0], kr_norm_g[0], w_uk[0], kn_norm_g[0], w_uv[0], sb_out_norm_g[0],
                      mla_out_norm_g[0], w_o[0], norm2_g[0], w_router_group[0], b_router_group[0],
                      w_router_expert[0], b_router_expert[0])
    wg, wu, wd = w_gate[0], w_up[0], w_down[0]

    cos_m, sin_m = _rope_tables(jnp.arange(N_META))
    cos_p, sin_p = _rope_tables(N_META + jnp.arange(seq))
    cos_s, sin_s = _rope_tables(jnp.tile(n_past + jnp.arange(nq), nb))
    xp2 = x_prompt.reshape(batch * seq, D_MODEL)
    xs2 = x_sample.reshape(nb * nq, D_MODEL)
    tm_s = min(ROW_TILE, nb * nq)
    pm = _project(meta_tokens, cos_m, sin_m, w, N_META)
    pp = _project(xp2, cos_p, sin_p, w, ROW_TILE)
    ps = _project(xs2, cos_s, sin_s, w, tm_s)

    sbo_p = _sb_prompt(pp['qsb'], pp['ksb16'], pp['vsb16'], _pad_rows(pm['ksb16'], BLK),
                       _pad_rows(pm['vsb16'], BLK), w['u2'], batch, seq)
    mlao_p = _mla_prompt(pp['qmla'], pp['kmla'], pp['vmla2'], _pad_rows(pm['kmla'], BLK),
                         _pad_rows(pm['vmla2'], BLK), batch, seq)

    kT = jnp.transpose(cache_sb_k, (0, 1, 3, 4, 2))
    vT = jnp.transpose(cache_sb_v, (0, 1, 3, 4, 2))
    krT = jnp.transpose(cache_mla_krope, (0, 1, 3, 2))
    qsb_g = ps['qsb'].reshape(nb, nq, SB_KV_HEADS, 2, SB_HEAD_DIM).transpose(0, 2, 3, 1, 4).reshape(
        nb, SB_KV_HEADS, 8, SB_HEAD_DIM)
    pad_keys = lambda a: jnp.pad(a.reshape(nb, nq, SB_KV_HEADS, SB_HEAD_DIM).transpose(0, 2, 1, 3),
                                 ((0, 0), (0, 0), (0, 8 - nq), (0, 0)))
    ksbn_g = pad_keys(ps['ksb'])
    vsbn_g = pad_keys(ps['vsb'])
    q_rows = jnp.tile(ps['qmla'].reshape(nb, nq, MLA_HEADS * LANES), (1, MLA_HEADS, 1))
    slab_of_lane = jnp.arange(MLA_HEADS * LANES)[None, :] // LANES
    head_of_row = jnp.arange(MLA_HEADS * nq)[:, None] // nq
    qbd = jnp.where((slab_of_lane == head_of_row)[None], q_rows, jnp.zeros((), BF16))
    kmn = jnp.pad(ps['kmla'].reshape(nb, nq, -1), ((0, 0), (0, 16 - nq), (0, 0)))
    vmn = jnp.pad(ps['vmla'].reshape(nb, nq, -1), ((0, 0), (0, 16 - nq), (0, 0)))
    sbo_g, mlao_full = _sample_attention(page_table, kT, vT, cache_mla_ckv, krT, qsb_g, ksbn_g, vsbn_g,
                                         qbd, kmn, vmn, w)
    sbo_s = sbo_g.reshape(nb, SB_KV_HEADS, 2, nq, SB_HEAD_DIM).transpose(0, 3, 1, 2, 4).reshape(
        nb * nq, SB_WIDTH)
    mf = mlao_full.reshape(nb, MLA_HEADS, nq, MLA_HEADS, MLA_V)
    mlao_s = jnp.stack([mf[:, h, :, h, :] for h in range(MLA_HEADS)], axis=2).reshape(nb * nq, MLA_WIDTH)

    def channel_mix(sbo, mlao, x2d, tm):
        h, t, ids, wts, counts = _post_attention(sbo, mlao, x2d, w, tm)
        return _moe(t, h, ids, wts, counts, wg, wu, wd, tm)

    y_prompt = channel_mix(sbo_p, mlao_p, xp2, ROW_TILE).reshape(batch, seq, D_MODEL)
    y_sample = channel_mix(sbo_s, mlao_s, xs2, tm_s).reshape(nb, nq, D_MODEL)

    def with_meta(m, p, tail):
        width = int(np.prod(tail))
        mb = jnp.broadcast_to(m[None, :, :width], (batch, N_META, width))
        full = jnp.concatenate([mb, p[:, :width].reshape(batch, seq, width)], axis=1)
        return full.reshape((1, batch, seq + N_META) + tail)

    kv_tail = (SB_KV_HEADS, SB_HEAD_DIM)
    return (y_prompt, y_sample,
            with_meta(pm['ksb'], pp['ksb'], kv_tail), with_meta(pm['vsb'], pp['vsb'], kv_tail),
            with_meta(pm['ckv'], pp['ckv'], (MLA_KV_LORA,)), with_meta(pm['kr'], pp['kr'], (MLA_ROPE,)),
            ps['ksb'].reshape((1, nb, nq) + kv_tail), ps['vsb'].reshape((1, nb, nq) + kv_tail),
            ps['ckv'].reshape(1, nb, nq, MLA_KV_LORA), ps['kr'][:, :MLA_ROPE].reshape(1, nb, nq, MLA_ROPE))
```

```python
import functools

import numpy as np
import jax
import jax.numpy as jnp
from jax import lax
from jax.experimental import pallas as pl
from jax.experimental.pallas import tpu as pltpu

F32 = jnp.float32
BF16 = jnp.bfloat16

D_MODEL = 1024
N_META = 16
EPS = 1e-6
SB_HEADS = 8
SB_KV_HEADS = 4
SB_HEAD_DIM = 64
SB_WIDTH = SB_HEADS * SB_HEAD_DIM
SB_SCALE = SB_HEAD_DIM ** -0.5
MLA_HEADS = 8
MLA_Q_LORA = 256
MLA_KV_LORA = 128
MLA_NOPE = 64
MLA_ROPE = 32
MLA_V = 64
MLA_WIDTH = MLA_HEADS * MLA_V
MLA_SCALE = (MLA_NOPE + MLA_ROPE) ** -0.5
ROPE_THETA = 10000.0
N_GROUPS = 4
EXPERTS_PER_GROUP = 8
N_EXPERTS = N_GROUPS * EXPERTS_PER_GROUP
D_EXPERT = 256

LANES = 128
BLK = 128
MLA_BLK = 256
PAGE = 128
CHUNK_PAGES = 32
SUB_PAGES = 8
ROW_TILE = 256
EXPERT_TILE = 256
VMEM_LIMIT = 56 * 1024 * 1024
SB_DEAD_LOG = -104.0
NEG_BIG = -1e30

_IN_COLS = 512 + 256 + 256 + 256 + 128 + 128 + 128


def _full(shape):
    nd = len(shape)
    return pl.BlockSpec(shape, lambda *_: (0,) * nd)


def _split2(x):
    hi = x.astype(BF16)
    lo = (x - hi.astype(F32)).astype(BF16)
    return hi, lo


def _split3(x):
    hi = x.astype(BF16)
    r = x - hi.astype(F32)
    mid = r.astype(BF16)
    lo = (r - mid.astype(F32)).astype(BF16)
    return hi, mid, lo


def _dot(a, b):
    return jnp.dot(a, b, preferred_element_type=F32)


def _dot_nt(a, b):
    return lax.dot_general(a, b, (((1,), (1,)), ((), ())), preferred_element_type=F32)


def _softplus(z):
    return jnp.maximum(z, 0.0) + jnp.log1p(jnp.exp(-jnp.abs(z)))


def _proj_body(x_ref, cos_ref, sin_ref, n1g_ref, win_ref, cqg_ref, ckvg_ref, wa_ref, wb_ref,
               sq_ref, sqt_ref, g1_ref, g2_ref, g3_ref, gk1_ref, gk3_ref, wukp_ref, sk_ref,
               skt_ref, wuv_ref, wuv2_ref,
               qsb_ref, ksb_ref, vsb_ref, ksb16_ref, vsb16_ref, ckv_ref, kr_ref, qmla_ref,
               kmla_ref, vmla_ref, vmla2_ref):
    x = x_ref[...]
    xn = x * lax.rsqrt(jnp.mean(x * x, axis=-1, keepdims=True) + EPS) * n1g_ref[...]
    p = _dot(xn.astype(BF16), win_ref[...])
    qsb_ref[...] = p[:, 0:512]
    ksb = p[:, 512:768]
    vsb = p[:, 768:1024]
    ksb_ref[...] = ksb
    vsb_ref[...] = vsb
    ksb16_ref[...] = ksb.astype(BF16)
    vsb16_ref[...] = vsb.astype(BF16)

    cq = p[:, 1024:1280]
    cq = cq * lax.rsqrt(jnp.mean(cq * cq, axis=-1, keepdims=True) + EPS) * cqg_ref[...]
    ckv = p[:, 1280:1408]
    ckv = ckv * lax.rsqrt(jnp.mean(ckv * ckv, axis=-1, keepdims=True) + EPS) * ckvg_ref[...]
    ckv_ref[...] = ckv

    cos = cos_ref[...]
    sin = sin_ref[...]
    s1 = p[:, 1408:1536]
    s2 = p[:, 1536:1664]
    inv_kr = lax.rsqrt(jnp.sum(s1 * s1, axis=-1, keepdims=True) * (1.0 / MLA_ROPE) + EPS)
    kr = (s1 * (cos * gk1_ref[...]) + s2 * (sin * gk3_ref[...])) * inv_kr
    kr_ref[...] = kr

    cqb = cq.astype(BF16)
    a = _dot(cqb, wa_ref[...])
    b = _dot(cqb, wb_ref[...])
    hi, lo = _split2(a * a)
    msq = _dot(hi, sq_ref[...]) + _dot(lo, sq_ref[...])
    ihi, ilo = _split2(lax.rsqrt(msq + EPS))
    invf = _dot(ihi, sqt_ref[...]) + _dot(ilo, sqt_ref[...])

    ckvb = ckv.astype(BF16)
    kraw = _dot(ckvb, wukp_ref[...])
    khi, klo = _split2(kraw * kraw)
    kmsq = _dot(khi, sk_ref[...]) + _dot(klo, sk_ref[...])
    kihi, kilo = _split2(lax.rsqrt(kmsq + EPS))
    kinvf = _dot(kihi, skt_ref[...]) + _dot(kilo, skt_ref[...])
    kr_shift = pltpu.roll(kr, 64, axis=1)

    for h in range(MLA_HEADS):
        sl = slice(h * LANES, (h + 1) * LANES)
        qh = (a[:, sl] * (g1_ref[:, sl] + cos * g2_ref[:, sl])
              + b[:, sl] * (sin * g3_ref[:, sl])) * invf[:, sl]
        qmla_ref[:, sl] = qh.astype(BF16)
        kmla_ref[:, sl] = (kraw[:, sl] * kinvf[:, sl] + kr_shift).astype(BF16)
    vmla_ref[...] = _dot(ckvb, wuv_ref[...]).astype(BF16)
    vmla2_ref[...] = _dot(ckvb, wuv2_ref[...]).astype(BF16)


def _project(x2d, cos_t, sin_t, w, tm):
    t = x2d.shape[0]
    tbl_blocks = cos_t.shape[0] // tm
    row = lambda width: pl.BlockSpec((tm, width), lambda i: (i, 0))
    tbl = pl.BlockSpec((tm, LANES), lambda i: (i % tbl_blocks, 0))
    consts = [w['n1g'], w['win'], w['cqg'], w['ckvg'], w['wa'], w['wb'], w['sq'], w['sqt'],
              w['g1'], w['g2'], w['g3'], w['gk1'], w['gk3'], w['wukp'], w['sk'], w['skt'], w['wuv'], w['wuv2']]
    out_widths = [(512, F32), (256, F32), (256, F32), (256, BF16), (256, BF16), (128, F32),
                  (128, F32), (1024, BF16), (1024, BF16), (512, BF16), (1024, BF16)]
    outs = pl.pallas_call(
        _proj_body,
        grid=(t // tm,),
        in_specs=[row(D_MODEL), tbl, tbl] + [_full(c.shape) for c in consts],
        out_specs=[row(wd) for wd, _ in out_widths],
        out_shape=[jax.ShapeDtypeStruct((t, wd), dt) for wd, dt in out_widths],
        compiler_params=pltpu.CompilerParams(dimension_semantics=("arbitrary",),
                                             vmem_limit_bytes=VMEM_LIMIT),
        name="proj",
    )(x2d, cos_t, sin_t, *consts)
    names = ['qsb', 'ksb', 'vsb', 'ksb16', 'vsb16', 'ckv', 'kr', 'qmla', 'kmla', 'vmla', 'vmla2']
    return dict(zip(names, outs))


def _sb_weights(z, vis, u, carry):
    n = z.shape[1]
    sp = _softplus(z)
    log_rest = -sp
    if vis is not None:
        log_rest = jnp.where(vis, log_rest, 0.0)
    hi, lo = _split2(log_rest)
    cs = _dot(hi, u) + _dot(lo, u)
    wgt = jnp.exp(z - sp + cs[:, :n] + carry)
    if vis is not None:
        wgt = jnp.where(vis, wgt, 0.0)
    return wgt.astype(BF16), cs[:, n:]


def _sb_block(q2, kj, vj, vis, u2, carry_ref, acc_ref, k_is_transposed=False):
    z = _dot(q2, kj) if k_is_transposed else _dot_nt(q2, kj)
    c = carry_ref[...]
    wb, total = _sb_weights(z, vis, u2, c)
    acc_ref[...] += _dot_nt(wb, vj) if k_is_transposed else _dot(wb, vj)
    carry_ref[...] = c + total


def _sb_prompt_body(q_ref, k_ref, v_ref, km_ref, vm_ref, u2_ref, o_ref, q2_ref, carry_ref, acc_ref):
    i = pl.program_id(1)
    lane = lax.broadcasted_iota(jnp.int32, (1, LANES), 1)
    lane_lo = lane < 64
    n_rows = SB_HEADS * BLK
    half = n_rows // 2
    row_t = lax.broadcasted_iota(jnp.int32, (n_rows, 1), 0) % BLK
    vis_diag = lane < row_t
    vis_meta = jnp.broadcast_to(lane < N_META, (n_rows, LANES))
    u2 = u2_ref[...]
    for g in range(SB_KV_HEADS):
        slab = q_ref[:, g * LANES:(g + 1) * LANES]
        rolled = pltpu.roll(slab, 64, axis=1)
        if g % 2 == 0:
            q_r0 = jnp.where(lane_lo, slab, 0.0)
            q_r1 = jnp.where(lane_lo, rolled, 0.0)
        else:
            q_r0 = jnp.where(lane_lo, 0.0, rolled)
            q_r1 = jnp.where(lane_lo, 0.0, slab)
        q2_ref[(2 * g) * BLK:(2 * g + 1) * BLK, :] = q_r0.astype(BF16)
        q2_ref[(2 * g + 1) * BLK:(2 * g + 2) * BLK, :] = q_r1.astype(BF16)
    carry_ref[...] = jnp.zeros_like(carry_ref)
    acc_ref[...] = jnp.zeros_like(acc_ref)

    def block(k_of_pair, v_of_pair, vis):
        z = jnp.concatenate([_dot_nt(q2_ref[0:half, :], k_of_pair(0)),
                             _dot_nt(q2_ref[half:, :], k_of_pair(1))], axis=0)
        c = carry_ref[...]
        wb, total = _sb_weights(z, vis, u2, c)
        acc_ref[0:half, :] += _dot(wb[0:half], v_of_pair(0))
        acc_ref[half:, :] += _dot(wb[half:], v_of_pair(1))
        carry_ref[...] = c + total

    def run(j, vis):
        rows = pl.ds(pl.multiple_of(j * BLK, BLK), BLK)
        block(lambda p: k_ref[rows, p * LANES:(p + 1) * LANES],
              lambda p: v_ref[rows, p * LANES:(p + 1) * LANES], vis)

    run(i, vis_diag)

    def cond(st):
        j, alive = st
        return jnp.logical_and(j >= 0, alive > SB_DEAD_LOG)

    def body(st):
        j, _ = st
        run(j, None)
        return j - 1, jnp.max(carry_ref[...])

    _, alive = lax.while_loop(cond, body, (i - 1, jnp.max(carry_ref[...])))

    @pl.when(alive > SB_DEAD_LOG)
    def _():
        block(lambda p: km_ref[:, p * LANES:(p + 1) * LANES],
              lambda p: vm_ref[:, p * LANES:(p + 1) * LANES], vis_meta)

    for g in range(SB_KV_HEADS):
        o_r0 = acc_ref[(2 * g) * BLK:(2 * g + 1) * BLK, :]
        o_r1 = acc_ref[(2 * g + 1) * BLK:(2 * g + 2) * BLK, :]
        if g % 2 == 0:
            out = jnp.where(lane_lo, o_r0, pltpu.roll(o_r1, 64, axis=1))
        else:
            out = jnp.where(lane_lo, pltpu.roll(o_r0, 64, axis=1), o_r1)
        o_ref[:, g * LANES:(g + 1) * LANES] = out


def _sb_prompt(qsb, ksb16, vsb16, km16, vm16, u2, batch, seq):
    nq = seq // BLK
    kv_w = SB_KV_HEADS * SB_HEAD_DIM
    return pl.pallas_call(
        _sb_prompt_body,
        grid=(batch, nq),
        in_specs=[pl.BlockSpec((BLK, SB_WIDTH), lambda b, i: (b * nq + i, 0)),
                  pl.BlockSpec((seq, kv_w), lambda b, i: (b, 0)),
                  pl.BlockSpec((seq, kv_w), lambda b, i: (b, 0)),
                  _full(km16.shape), _full(vm16.shape), _full(u2.shape)],
        out_specs=pl.BlockSpec((BLK, SB_WIDTH), lambda b, i: (b * nq + i, 0)),
        out_shape=jax.ShapeDtypeStruct((batch * seq, SB_WIDTH), F32),
        scratch_shapes=[pltpu.VMEM((SB_HEADS * BLK, LANES), BF16), pltpu.VMEM((SB_HEADS * BLK, LANES), F32),
                        pltpu.VMEM((SB_HEADS * BLK, LANES), F32)],
        compiler_params=pltpu.CompilerParams(dimension_semantics=("arbitrary", "arbitrary"),
                                             vmem_limit_bytes=VMEM_LIMIT),
        name="sb_prompt",
    )(qsb, ksb16, vsb16, km16, vm16, u2)


def _mla_prompt_body(q_ref, k_ref, v2_ref, km_ref, vm2_ref, o_ref, m_ref, l_ref, acc_ref):
    i = pl.program_id(1)
    tq = MLA_BLK
    lane = lax.broadcasted_iota(jnp.int32, (1, LANES), 1)
    lane_lo = lane < 64
    col = lax.broadcasted_iota(jnp.int32, (1, tq), 1)
    row_t = lax.broadcasted_iota(jnp.int32, (tq, 1), 0)
    vis_diag = col <= row_t
    vis_meta = jnp.broadcast_to(lane < N_META, (tq, LANES))
    m_ref[...] = jnp.full_like(m_ref, NEG_BIG)
    l_ref[...] = jnp.zeros_like(l_ref)
    acc_ref[...] = jnp.zeros_like(acc_ref)

    def block(load_k, load_v2, vis):
        for hp in range(MLA_HEADS // 2):
            ps, alphas = [], []
            for par in range(2):
                h = 2 * hp + par
                s = _dot_nt(q_ref[:, h * LANES:(h + 1) * LANES], load_k(h))
                if vis is not None:
                    s = jnp.where(vis, s, NEG_BIG)
                m_old = m_ref[h]
                m_new = jnp.maximum(m_old, jnp.max(s, axis=-1, keepdims=True))
                p = jnp.exp(s - m_new)
                alpha = jnp.exp(m_old - m_new)
                l_ref[h] = alpha * l_ref[h] + jnp.sum(p, axis=-1, keepdims=True)
                m_ref[h] = m_new
                ps.append(p.astype(BF16))
                alphas.append(alpha)
            pcat = jnp.concatenate(ps, axis=1)
            vbd = jnp.concatenate([load_v2(2 * hp), load_v2(2 * hp + 1)], axis=0)
            acc_ref[hp] = acc_ref[hp] * jnp.where(lane_lo, alphas[0], alphas[1]) + _dot(pcat, vbd)

    block(lambda h: km_ref[:, h * LANES:(h + 1) * LANES],
          lambda c: vm2_ref[:, c * LANES:(c + 1) * LANES], vis_meta)

    def rows_of(j):
        return pl.ds(pl.multiple_of(j * tq, tq), tq)

    def body(j, carry):
        rows = rows_of(j)
        block(lambda h: k_ref[rows, h * LANES:(h + 1) * LANES],
              lambda c: v2_ref[rows, c * LANES:(c + 1) * LANES], None)
        return carry

    lax.fori_loop(0, i, body, 0)
    rows = rows_of(i)
    block(lambda h: k_ref[rows, h * LANES:(h + 1) * LANES],
          lambda c: v2_ref[rows, c * LANES:(c + 1) * LANES], vis_diag)
    for hp in range(MLA_HEADS // 2):
        o_ref[:, hp * LANES:(hp + 1) * LANES] = acc_ref[hp] / jnp.where(lane_lo, l_ref[2 * hp], l_ref[2 * hp + 1])


def _mla_prompt(qmla, kmla, vmla2, kmm, vmm2, batch, seq):
    nq = seq // MLA_BLK
    width = MLA_HEADS * LANES
    return pl.pallas_call(
        _mla_prompt_body,
        grid=(batch, nq),
        in_specs=[pl.BlockSpec((MLA_BLK, width), lambda b, i: (b * nq + i, 0)),
                  pl.BlockSpec((seq, width), lambda b, i: (b, 0)),
                  pl.BlockSpec((seq, width), lambda b, i: (b, 0)),
                  _full(kmm.shape), _full(vmm2.shape)],
        out_specs=pl.BlockSpec((MLA_BLK, MLA_WIDTH), lambda b, i: (b * nq + i, 0)),
        out_shape=jax.ShapeDtypeStruct((batch * seq, MLA_WIDTH), F32),
        scratch_shapes=[pltpu.VMEM((MLA_HEADS, MLA_BLK, 1), F32), pltpu.VMEM((MLA_HEADS, MLA_BLK, 1), F32),
                        pltpu.VMEM((MLA_HEADS // 2, MLA_BLK, LANES), F32)],
        compiler_params=pltpu.CompilerParams(dimension_semantics=("arbitrary", "arbitrary"),
                                             vmem_limit_bytes=VMEM_LIMIT),
        name="mla_prompt",
    )(qmla, kmla, vmla2, kmm, vmm2)


def _sample_body(pt_ref,
                 kT_hbm, vT_hbm, ckv_hbm, krT_hbm,
                 qsb_ref, ksbn_ref, vsbn_ref, qbd_ref, kmn_ref, vmn_ref,
                 wukt_ref, wabs_ref, shs_ref, wuv_ref, u2_ref, u3_ref,
                 sbo_ref, mlao_ref,
                 cbuf, rbuf, sbk, sbv, wq_ref, qr_ref, m_ref, l_ref, lat_ref,
                 carry_ref, sbacc_ref, sem_c, sem_r, sem_sb):
    b = pl.program_id(0)
    c = pl.program_id(1)
    nb = pl.num_programs(0)
    nc = pl.num_programs(1)
    n_pages = nc * CHUNK_PAGES
    step = b * nc + c
    slot = step % 2
    chunk = CHUNK_PAGES * PAGE
    rows_q = MLA_HEADS * 4

    def chunk_copies(bb, cc, sl):
        cps = []
        for pg in range(CHUNK_PAGES):
            page = pt_ref[bb, cc * CHUNK_PAGES + pg]
            cps.append(pltpu.make_async_copy(
                ckv_hbm.at[0, page], cbuf.at[sl, pl.ds(pg * PAGE, PAGE), :], sem_c.at[sl]))
            cps.append(pltpu.make_async_copy(
                krT_hbm.at[0, page], rbuf.at[sl, :, pl.ds(pg * PAGE, PAGE)], sem_r.at[sl]))
        return cps

    def sb_copies(bb, page_idx, sl):
        page = pt_ref[bb, page_idx]
        return [pltpu.make_async_copy(kT_hbm.at[0, page], sbk.at[sl], sem_sb.at[sl, 0]),
                pltpu.make_async_copy(vT_hbm.at[0, page], sbv.at[sl], sem_sb.at[sl, 1])]

    @pl.when(step == 0)
    def _():
        for cp in chunk_copies(0, 0, 0):
            cp.start()

    @pl.when(c == 0)
    def _():
        for sl in range(2):
            for cp in sb_copies(b, n_pages - 1 - sl, sl):
                cp.start()
        qa = _dot(qbd_ref[0], wabs_ref[...])
        wq_ref[0:512, :] = wukt_ref[...]
        wq_ref[512:512 + rows_q, :] = qa[:, :LANES].astype(BF16)
        qr_ref[...] = qa[:, LANES:LANES + MLA_ROPE].astype(BF16)
        m_ref[...] = jnp.full_like(m_ref, NEG_BIG)
        l_ref[...] = jnp.zeros_like(l_ref)
        lat_ref[...] = jnp.zeros_like(lat_ref)

    for cp in chunk_copies(b, c, slot):
        cp.wait()

    @pl.when(step + 1 < nb * nc)
    def _():
        nxt = step + 1
        for cp in chunk_copies(nxt // nc, nxt % nc, 1 - slot):
            cp.start()

    sub = SUB_PAGES * PAGE
    cbs, scores = [], []
    for u in range(CHUNK_PAGES // SUB_PAGES):
        cols = slice(u * sub, (u + 1) * sub)
        cb = cbuf[slot, cols, :].astype(BF16)
        big = _dot_nt(wq_ref[...], cb)
        k2 = big[0:512] * big[0:512]
        part = k2.reshape(MLA_HEADS, MLA_NOPE // 8, 8, sub).sum(axis=1).reshape(MLA_HEADS * 8, sub)
        ksum = _dot(shs_ref[...], part.astype(BF16))
        scores.append(big[512:512 + rows_q] * lax.rsqrt(ksum + EPS)
                      + _dot(qr_ref[...], rbuf[slot, :, cols].astype(BF16)))
        cbs.append(cb)
    s = jnp.concatenate(scores, axis=1)
    m_old = m_ref[...]
    m_new = jnp.maximum(m_old, jnp.max(s, axis=-1, keepdims=True))
    p = jnp.exp(s - m_new).astype(BF16)
    alpha = jnp.exp(m_old - m_new)
    l_ref[...] = alpha * l_ref[...] + jnp.sum(p.astype(F32), axis=-1, keepdims=True)
    pv = _dot(p[:, 0:sub], cbs[0])
    for u in range(1, len(cbs)):
        pv += _dot(p[:, u * sub:(u + 1) * sub], cbs[u])
    lat_ref[...] = alpha * lat_ref[...] + pv
    m_ref[...] = m_new

    @pl.when(c == nc - 1)
    def _():
        lane8 = lax.broadcasted_iota(jnp.int32, (rows_q, 16), 1)
        q_of_row = lax.broadcasted_iota(jnp.int32, (rows_q, 16), 0) % 4
        s_new = _dot_nt(qbd_ref[0], kmn_ref[0])
        s_new = jnp.where(lane8 <= q_of_row, s_new, NEG_BIG)
        m_old = m_ref[...]
        m_fin = jnp.maximum(m_old, jnp.max(s_new, axis=-1, keepdims=True))
        p_new = jnp.exp(s_new - m_fin)
        alpha = jnp.exp(m_old - m_fin)
        l_fin = alpha * l_ref[...] + jnp.sum(p_new, axis=-1, keepdims=True)
        o = _dot((alpha * lat_ref[...]).astype(BF16), wuv_ref[...]) + _dot(p_new.astype(BF16), vmn_ref[0])
        mlao_ref[0] = o / l_fin

        u2 = u2_ref[...]
        n_first = 3 * PAGE
        lane3 = lax.broadcasted_iota(jnp.int32, (1, n_first), 1)
        q_row = lax.broadcasted_iota(jnp.int32, (SB_KV_HEADS * 8, 1), 0) % 4
        vis_first = lane3 < 2 * PAGE + q_row
        zpad = jnp.zeros((PAGE - 8, SB_HEAD_DIM), F32)
        for sl in range(2):
            for cp in sb_copies(b, n_pages - 1 - sl, sl):
                cp.wait()
        vns, zs = [], []
        for g in range(SB_KV_HEADS):
            qg = qsb_ref[0, g].astype(BF16)
            kn = jnp.concatenate([ksbn_ref[0, g], zpad], axis=0).astype(BF16)
            vn = jnp.concatenate([vsbn_ref[0, g], zpad], axis=0).astype(BF16)
            zs.append(jnp.concatenate([_dot(qg, sbk[1, g].astype(BF16)), _dot(qg, sbk[0, g].astype(BF16)),
                                       _dot_nt(qg, kn)], axis=1))
            vns.append(vn)
        z = jnp.concatenate(zs, axis=0)
        wb, total = _sb_weights(z, vis_first, u3_ref[...], 0.0)
        carry_ref[...] = total
        for g in range(SB_KV_HEADS):
            wg = wb[g * 8:(g + 1) * 8]
            sbacc_ref[g * 8:(g + 1) * 8, :] = (
                _dot_nt(wg[:, 0:PAGE], sbv[1, g].astype(BF16))
                + _dot_nt(wg[:, PAGE:2 * PAGE], sbv[0, g].astype(BF16))
                + _dot(wg[:, 2 * PAGE:], vns[g]))

        def sb_page(sl):
            for g in range(SB_KV_HEADS):
                qg = qsb_ref[0, g].astype(BF16)
                rows = slice(g * 8, (g + 1) * 8)
                _sb_block(qg, sbk[sl, g].astype(BF16), sbv[sl, g].astype(BF16), None, u2,
                          carry_ref.at[rows], sbacc_ref.at[rows], k_is_transposed=True)

        def cond(st):
            j, alive = st
            return jnp.logical_and(j >= 0, alive > SB_DEAD_LOG)

        def body(st):
            j, _ = st
            cps = sb_copies(b, j, 0)
            for cp in cps:
                cp.start()
            for cp in cps:
                cp.wait()
            sb_page(0)
            return j - 1, jnp.max(carry_ref[...])

        lax.while_loop(cond, body, (n_pages - 3, jnp.max(carry_ref[...])))
        for g in range(SB_KV_HEADS):
            sbo_ref[0, g] = sbacc_ref[g * 8:(g + 1) * 8, :]


def _sample_attention(page_table, kT, vT, ckv_cache, krT, qsb_g, ksbn_g, vsbn_g, qbd, kmn, vmn, w):
    nb, n_pages = page_table.shape
    nc = n_pages // CHUNK_PAGES
    chunk = CHUNK_PAGES * PAGE
    rows_q = MLA_HEADS * 4
    per_b = lambda shape: pl.BlockSpec((1,) + shape, lambda b, c, pt: (b,) + (0,) * len(shape))
    const = lambda a: pl.BlockSpec(a.shape, lambda b, c, pt: (0,) * a.ndim)
    consts = [w['wukt'], w['wabs'], w['shs'], w['wuv'], w['u2'], w['u3']]
    any_spec = pl.BlockSpec(memory_space=pl.ANY)
    return pl.pallas_call(
        _sample_body,
        grid_spec=pltpu.PrefetchScalarGridSpec(
            num_scalar_prefetch=1,
            grid=(nb, nc),
            in_specs=[any_spec, any_spec, any_spec, any_spec,
                      per_b((SB_KV_HEADS, 8, SB_HEAD_DIM)), per_b((SB_KV_HEADS, 8, SB_HEAD_DIM)),
                      per_b((SB_KV_HEADS, 8, SB_HEAD_DIM)), per_b((rows_q, MLA_HEADS * LANES)),
                      per_b((16, MLA_HEADS * LANES)), per_b((16, MLA_WIDTH))]
                     + [const(a) for a in consts],
            out_specs=[per_b((SB_KV_HEADS, 8, SB_HEAD_DIM)), per_b((rows_q, MLA_WIDTH))],
            scratch_shapes=[
                pltpu.VMEM((2, chunk, MLA_KV_LORA), F32),
                pltpu.VMEM((2, MLA_ROPE, chunk), F32),
                pltpu.VMEM((2, SB_KV_HEADS, SB_HEAD_DIM, PAGE), F32),
                pltpu.VMEM((2, SB_KV_HEADS, SB_HEAD_DIM, PAGE), F32),
                pltpu.VMEM((512 + rows_q, MLA_KV_LORA), BF16),
                pltpu.VMEM((rows_q, MLA_ROPE), BF16),
                pltpu.VMEM((rows_q, 1), F32),
                pltpu.VMEM((rows_q, 1), F32),
                pltpu.VMEM((rows_q, MLA_KV_LORA), F32),
                pltpu.VMEM((SB_KV_HEADS * 8, LANES), F32),
                pltpu.VMEM((SB_KV_HEADS * 8, SB_HEAD_DIM), F32),
                pltpu.SemaphoreType.DMA((2,)),
                pltpu.SemaphoreType.DMA((2,)),
                pltpu.SemaphoreType.DMA((2, 2)),
            ]),
        out_shape=[jax.ShapeDtypeStruct((nb, SB_KV_HEADS, 8, SB_HEAD_DIM), F32),
                   jax.ShapeDtypeStruct((nb, rows_q, MLA_WIDTH), F32)],
        compiler_params=pltpu.CompilerParams(dimension_semantics=("arbitrary", "arbitrary"),
                                             vmem_limit_bytes=VMEM_LIMIT),
        name="sample_attn",
    )(page_table, kT, vT, ckv_cache, krT, qsb_g, ksbn_g, vsbn_g, qbd, kmn, vmn, *consts)


def _post_body(sbo_ref, mlao_ref, x_ref, gsb_ref, gmla_ref, wo_ref, n2g_ref, wrh_ref, wrl_ref,
               br_ref, ltri_ref, h_ref, t_ref, ids_ref, wts_ref, counts_ref, cnt_ref):
    sbo = sbo_ref[...]
    mlao = mlao_ref[...]
    m_sb = sbo * lax.rsqrt(jnp.mean(sbo * sbo, axis=-1, keepdims=True) + EPS) * gsb_ref[...]
    m_mla = mlao * lax.rsqrt(jnp.mean(mlao * mlao, axis=-1, keepdims=True) + EPS) * gmla_ref[...]
    h = (x_ref[...] + _dot(m_sb.astype(BF16), wo_ref[0:SB_WIDTH, :])
         + _dot(m_mla.astype(BF16), wo_ref[SB_WIDTH:, :]))
    h_ref[...] = h
    t = h * lax.rsqrt(jnp.mean(h * h, axis=-1, keepdims=True) + EPS) * n2g_ref[...]
    t_ref[...] = t
    thi, tlo = _split2(t)
    lg = (_dot(thi, wrh_ref[...]) + _dot(tlo, wrh_ref[...]) + _dot(thi, wrl_ref[...])
          + br_ref[...])
    tm = lg.shape[0]
    lane_i = lax.broadcasted_iota(jnp.int32, (tm, LANES), 1)
    lane = lane_i.astype(F32)
    big_i = jnp.float32(1 << 20)
    is_grp = lane < N_GROUPS
    mg = jnp.max(jnp.where(is_grp, lg, -jnp.inf), axis=-1, keepdims=True)
    gidx = jnp.min(jnp.where(is_grp & (lg == mg), lane, big_i), axis=-1, keepdims=True)
    p_grp = 1.0 / jnp.sum(jnp.where(is_grp, jnp.exp(lg - mg), 0.0), axis=-1, keepdims=True)
    lo_lane = N_GROUPS + gidx * EXPERTS_PER_GROUP
    sel = (lane >= lo_lane) & (lane < lo_lane + EXPERTS_PER_GROUP)
    v1 = jnp.max(jnp.where(sel, lg, -jnp.inf), axis=-1, keepdims=True)
    i1 = jnp.min(jnp.where(sel & (lg == v1), lane, big_i), axis=-1, keepdims=True)
    sel2 = sel & (lane != i1)
    v2 = jnp.max(jnp.where(sel2, lg, -jnp.inf), axis=-1, keepdims=True)
    i2 = jnp.min(jnp.where(sel2 & (lg == v2), lane, big_i), axis=-1, keepdims=True)
    e21 = jnp.exp(v2 - v1)
    w1 = p_grp / (1.0 + e21)
    w2 = w1 * e21
    e1 = i1 - N_GROUPS
    e2 = i2 - N_GROUPS
    @pl.when(pl.program_id(0) == 0)
    def _():
        cnt_ref[...] = jnp.zeros_like(cnt_ref)

    oh1 = jnp.where(lane == e1, 1.0, 0.0)
    oh2 = jnp.where(lane == e2, 1.0, 0.0)
    both = oh1 + oh2
    before = _dot(ltri_ref[...], both.astype(BF16)) + cnt_ref[0:1, :]
    r1 = jnp.sum(oh1 * before, axis=-1, keepdims=True)
    r2 = jnp.sum(oh2 * before, axis=-1, keepdims=True)
    cnt_new = cnt_ref[...] + jnp.sum(both, axis=0, keepdims=True)
    cnt_ref[...] = cnt_new
    counts_ref[...] = cnt_new
    ids = jnp.where(lane_i == 0, e1, jnp.where(lane_i == 1, e2,
                    jnp.where(lane_i == 2, r1, jnp.where(lane_i == 3, r2, 0.0))))
    ids_ref[...] = ids.astype(jnp.int32)
    wts_ref[...] = jnp.where(lane_i == 0, w1, jnp.where(lane_i == 1, w2, 0.0))


def _post_attention(sbo, mlao, x2d, w, tm):
    t = x2d.shape[0]
    row = lambda width: pl.BlockSpec((tm, width), lambda i: (i, 0))
    ltri = jnp.asarray(np.tril(np.ones((tm, tm), np.float32), -1), dtype=BF16)
    consts = [w['gsb'], w['gmla'], w['wo'], w['n2g'], w['wrh'], w['wrl'], w['br'], ltri]
    return pl.pallas_call(
        _post_body,
        grid=(t // tm,),
        in_specs=[row(SB_WIDTH), row(MLA_WIDTH), row(D_MODEL)] + [_full(c.shape) for c in consts],
        out_specs=[row(D_MODEL), row(D_MODEL), row(LANES), row(LANES), _full((8, LANES))],
        out_shape=[jax.ShapeDtypeStruct((t, D_MODEL), F32), jax.ShapeDtypeStruct((t, D_MODEL), F32),
                   jax.ShapeDtypeStruct((t, LANES), jnp.int32), jax.ShapeDtypeStruct((t, LANES), F32),
                   jax.ShapeDtypeStruct((8, LANES), F32)],
        scratch_shapes=[pltpu.VMEM((8, LANES), F32)],
        compiler_params=pltpu.CompilerParams(dimension_semantics=("arbitrary",),
                                             vmem_limit_bytes=VMEM_LIMIT),
        name="post_attn",
    )(sbo, mlao, x2d, *consts)


def _wait_rows(src, dst, sem, cnt):
    def wait8(_, c):
        pltpu.make_async_copy(src.at[pl.ds(0, 8), :], dst.at[pl.ds(0, 8), :], sem).wait()
        return c

    def wait1(_, c):
        pltpu.make_async_copy(src.at[pl.ds(0, 1), :], dst.at[pl.ds(0, 1), :], sem).wait()
        return c

    lax.fori_loop(0, cnt // 8, wait8, 0)
    lax.fori_loop(0, cnt % 8, wait1, 0)


def _dispatch_body(pos_ref, tile_cnt_ref, npad_ref, t_ref, xs_hbm, zbuf, sem, sem_pad, sem_tile):
    i = pl.program_id(0)
    tm = t_ref.shape[0]

    def empty_tile_copy(n):
        return pltpu.make_async_copy(zbuf, xs_hbm.at[pl.ds(n * EXPERT_TILE, EXPERT_TILE), :], sem_tile)

    @pl.when(i == 0)
    def _():
        zbuf[...] = jnp.zeros_like(zbuf)

        def per_tile(n, c):
            cnt = tile_cnt_ref[n]

            @pl.when(cnt == 0)
            def _():
                empty_tile_copy(n).start()

            def per_row(r, c2):
                pltpu.make_async_copy(zbuf.at[pl.ds(0, 1), :],
                                      xs_hbm.at[pl.ds(n * EXPERT_TILE + r, 1), :], sem_pad).start()
                return c2

            lax.fori_loop(jnp.where(cnt > 0, cnt, EXPERT_TILE), EXPERT_TILE, per_row, 0)
            return c

        lax.fori_loop(0, tile_cnt_ref.shape[0], per_tile, 0)

    def send(r, c):
        a = 2 * (i * tm + r)
        for k in range(2):
            pltpu.make_async_copy(t_ref.at[pl.ds(r, 1), :],
                                  xs_hbm.at[pl.ds(pos_ref[a + k], 1), :], sem).start()
        return c

    lax.fori_loop(0, tm, send, 0, unroll=4)
    for _ in range(2):
        pltpu.make_async_copy(t_ref, xs_hbm.at[pl.ds(0, tm), :], sem).wait()

    @pl.when(i == pl.num_programs(0) - 1)
    def _():
        _wait_rows(zbuf, xs_hbm, sem_pad, npad_ref[0])

        def wait_tile(_, c):
            empty_tile_copy(0).wait()
            return c

        lax.fori_loop(0, npad_ref[1], wait_tile, 0)


def _dispatch(t2d, pos_flat, tile_cnt, n_pad, n_rows, tm):
    t = t2d.shape[0]
    return pl.pallas_call(
        _dispatch_body,
        grid_spec=pltpu.PrefetchScalarGridSpec(
            num_scalar_prefetch=3,
            grid=(t // tm,),
            in_specs=[pl.BlockSpec((tm, D_MODEL), lambda i, *_: (i, 0))],
            out_specs=pl.BlockSpec(memory_space=pl.ANY),
            scratch_shapes=[pltpu.VMEM((EXPERT_TILE, D_MODEL), F32), pltpu.SemaphoreType.DMA(()),
                            pltpu.SemaphoreType.DMA(()), pltpu.SemaphoreType.DMA(())]),
        out_shape=jax.ShapeDtypeStruct((n_rows, D_MODEL), F32),
        compiler_params=pltpu.CompilerParams(dimension_semantics=("arbitrary",),
                                             vmem_limit_bytes=VMEM_LIMIT),
        name="moe_dispatch",
    )(pos_flat, tile_cnt, n_pad, t2d)


def _moe_body(tile_e_ref, tile_src_ref, tile_cnt_ref, x_ref, wg_ref, wu_ref, wd_ref, y_ref):
    cnt = tile_cnt_ref[pl.program_id(0)]

    @pl.when(cnt == 0)
    def _():
        y_ref[...] = jnp.zeros_like(y_ref)

    @pl.when(cnt > 0)
    def _():
        xb = x_ref[...].astype(BF16)
        gate = _dot(xb, wg_ref[0].astype(BF16))
        up = _dot(xb, wu_ref[0].astype(BF16))
        hid = gate * jax.nn.sigmoid(gate) * up
        y_ref[...] = _dot(hid.astype(BF16), wd_ref[0].astype(BF16))


def _experts(xs, tile_e, tile_src, tile_cnt, w_gate, w_up, w_down):
    n_tiles = tile_e.shape[0]
    wspec = lambda shape: pl.BlockSpec((1,) + shape, lambda n, te, ts, tc: (te[n], 0, 0))
    rows = pl.BlockSpec((EXPERT_TILE, D_MODEL), lambda n, te, ts, tc: (ts[n], 0))
    return pl.pallas_call(
        _moe_body,
        grid_spec=pltpu.PrefetchScalarGridSpec(
            num_scalar_prefetch=3,
            grid=(n_tiles,),
            in_specs=[rows, wspec((D_MODEL, D_EXPERT)), wspec((D_MODEL, D_EXPERT)),
                      wspec((D_EXPERT, D_MODEL))],
            out_specs=pl.BlockSpec((EXPERT_TILE, D_MODEL), lambda n, te, ts, tc: (n, 0))),
        out_shape=jax.ShapeDtypeStruct(xs.shape, F32),
        compiler_params=pltpu.CompilerParams(dimension_semantics=("arbitrary",),
                                             vmem_limit_bytes=VMEM_LIMIT),
        name="moe",
    )(tile_e, tile_src, tile_cnt, xs, w_gate, w_up, w_down)


def _combine_body(pos_ref, h_ref, wts_ref, ys_hbm, o_ref, y0buf, y1buf, sem):
    i = pl.program_id(0)
    tm = h_ref.shape[0]

    def fetch(r, c):
        a = 2 * (i * tm + r)
        pltpu.make_async_copy(ys_hbm.at[pl.ds(pos_ref[a], 1), :], y0buf.at[pl.ds(r, 1), :], sem).start()
        pltpu.make_async_copy(ys_hbm.at[pl.ds(pos_ref[a + 1], 1), :], y1buf.at[pl.ds(r, 1), :], sem).start()
        return c

    lax.fori_loop(0, tm, fetch, 0, unroll=4)
    pltpu.make_async_copy(ys_hbm.at[pl.ds(0, tm), :], y0buf, sem).wait()
    pltpu.make_async_copy(ys_hbm.at[pl.ds(0, tm), :], y1buf, sem).wait()
    wts = wts_ref[...]
    o_ref[...] = h_ref[...] + wts[:, 0:1] * y0buf[...] + wts[:, 1:2] * y1buf[...]


def _combine(h2d, wts, ys, pos_flat, tm):
    t = h2d.shape[0]
    return pl.pallas_call(
        _combine_body,
        grid_spec=pltpu.PrefetchScalarGridSpec(
            num_scalar_prefetch=1,
            grid=(t // tm,),
            in_specs=[pl.BlockSpec((tm, D_MODEL), lambda i, p: (i, 0)),
                      pl.BlockSpec((tm, LANES), lambda i, p: (i, 0)),
                      pl.BlockSpec(memory_space=pl.ANY)],
            out_specs=pl.BlockSpec((tm, D_MODEL), lambda i, p: (i, 0)),
            scratch_shapes=[pltpu.VMEM((tm, D_MODEL), F32), pltpu.VMEM((tm, D_MODEL), F32),
                            pltpu.SemaphoreType.DMA(())]),
        out_shape=jax.ShapeDtypeStruct((t, D_MODEL), F32),
        compiler_params=pltpu.CompilerParams(dimension_semantics=("arbitrary",),
                                             vmem_limit_bytes=VMEM_LIMIT),
        name="moe_combine",
    )(pos_flat, h2d, wts, ys)


def _moe(t2d, h2d, ids, wts, counts_f, w_gate, w_up, w_down, tm):
    t = t2d.shape[0]
    n_tiles = (2 * t) // EXPERT_TILE + N_EXPERTS
    counts = counts_f[0, :N_EXPERTS].astype(jnp.int32)
    tiles_e = (counts + EXPERT_TILE - 1) // EXPERT_TILE
    tile_end = jnp.cumsum(tiles_e)
    tile_start = tile_end - tiles_e
    total = tile_end[-1]
    tile_ids = jnp.arange(n_tiles, dtype=jnp.int32)
    tile_e = jnp.minimum(jnp.sum((tile_ids[:, None] >= tile_end[None, :]).astype(jnp.int32), axis=1),
                         N_EXPERTS - 1)
    onehot_t = (tile_e[:, None] == jnp.arange(N_EXPERTS)[None, :]).astype(jnp.int32)
    left = jnp.sum(onehot_t * (counts[None, :] - (tile_ids[:, None] - tile_start[None, :]) * EXPERT_TILE), axis=1)
    tile_cnt = jnp.where(tile_ids < total, jnp.clip(left, 0, EXPERT_TILE), 0).astype(jnp.int32)
    tile_src = jnp.minimum(tile_ids, total - 1).astype(jnp.int32)
    n_pad = jnp.stack([jnp.sum(jnp.where(tile_cnt > 0, EXPERT_TILE - tile_cnt, 0)),
                       jnp.sum((tile_cnt == 0).astype(jnp.int32))]).astype(jnp.int32)
    experts = ids[:, 0:2]
    onehot_a = (experts[:, :, None] == jnp.arange(N_EXPERTS)[None, None, :]).astype(jnp.int32)
    pos = jnp.sum(onehot_a * tile_start[None, None, :], axis=-1) * EXPERT_TILE + ids[:, 2:4]
    pos_flat = pos.reshape(-1).astype(jnp.int32)
    xs = _dispatch(t2d, pos_flat, tile_cnt, n_pad, n_tiles * EXPERT_TILE, tm)
    ys = _experts(xs, tile_e.astype(jnp.int32), tile_src, tile_cnt, w_gate, w_up, w_down)
    return _combine(h2d, wts, ys, pos_flat, tm)


def _rot_half_cols(wr):
    half = MLA_ROPE // 2
    return jnp.concatenate([-wr[..., half:], wr[..., :half]], axis=-1)


def _swap_halves(g):
    half = MLA_ROPE // 2
    return jnp.concatenate([g[..., half:], g[..., :half]], axis=-1)


def _group_sum_consts():
    sq = np.zeros((MLA_HEADS * LANES, LANES), np.float32)
    sqt = np.zeros((LANES, MLA_HEADS * LANES), np.float32)
    sk = np.zeros((MLA_HEADS * LANES, LANES), np.float32)
    skt = np.zeros((LANES, MLA_HEADS * LANES), np.float32)
    for h in range(MLA_HEADS):
        nope = slice(h * LANES, h * LANES + MLA_NOPE)
        rope = slice(h * LANES + MLA_NOPE, h * LANES + MLA_NOPE + MLA_ROPE)
        sq[nope, 2 * h] = 1.0 / MLA_NOPE
        sq[rope, 2 * h + 1] = 1.0 / MLA_ROPE
        sqt[2 * h, nope] = 1.0
        sqt[2 * h + 1, rope] = 1.0
        sk[nope, h] = 1.0 / MLA_NOPE
        skt[h, nope] = 1.0
    u2 = np.zeros((LANES, 2 * LANES), np.float32)
    u2[:, :LANES] = (np.arange(LANES)[:, None] > np.arange(LANES)[None, :]).astype(np.float32)
    u2[:, LANES:] = 1.0
    n3 = 3 * LANES
    u3 = np.ones((n3, n3 + LANES), np.float32)
    u3[:, :n3] = (np.arange(n3)[:, None] > np.arange(n3)[None, :]).astype(np.float32)
    shs = np.zeros((MLA_HEADS * 4, MLA_HEADS * 8), np.float32)
    for h in range(MLA_HEADS):
        shs[h * 4:(h + 1) * 4, h * 8:(h + 1) * 8] = 1.0 / MLA_NOPE
    as16 = lambda a: jnp.asarray(a, dtype=BF16)
    return dict(sq=as16(sq), sqt=as16(sqt), sk=as16(sk), skt=as16(skt), u2=as16(u2), u3=as16(u3), shs=as16(shs))


def _prep_weights(norm1_g, w_in, cq_norm_g, ckv_norm_g, w_uq, qn_norm_g, qr_norm_g, kr_norm_g,
                  w_uk, kn_norm_g, w_uv, sb_out_norm_g, mla_out_norm_g, w_o, norm2_g,
                  w_router_group, b_router_group, w_router_expert, b_router_expert):
    w = _group_sum_consts()
    wr = w_in[:, 1408:1440]
    pad96 = jnp.zeros((D_MODEL, LANES - MLA_ROPE), F32)
    w['win'] = jnp.concatenate([w_in[:, :512] * SB_SCALE, w_in[:, 512:1408], wr, pad96,
                                _rot_half_cols(wr), pad96], axis=1).astype(BF16)
    w['n1g'] = norm1_g[None, :]
    w['cqg'] = cq_norm_g[None, :]
    w['ckvg'] = ckv_norm_g[None, :]
    w['wa'] = jnp.pad(w_uq, ((0, 0), (0, 0), (0, LANES - MLA_NOPE - MLA_ROPE))).reshape(
        MLA_Q_LORA, MLA_HEADS * LANES).astype(BF16)
    w['wb'] = jnp.pad(_rot_half_cols(w_uq[:, :, MLA_NOPE:]),
                      ((0, 0), (0, 0), (MLA_NOPE, LANES - MLA_NOPE - MLA_ROPE))).reshape(
        MLA_Q_LORA, MLA_HEADS * LANES).astype(BF16)
    z32 = jnp.zeros((LANES - MLA_NOPE - MLA_ROPE,), F32)
    z64 = jnp.zeros((MLA_NOPE,), F32)
    z96 = jnp.zeros((LANES - MLA_ROPE,), F32)
    head_row = lambda v: jnp.tile(v, MLA_HEADS)[None, :]
    w['g1'] = head_row(jnp.concatenate([qn_norm_g * kn_norm_g * MLA_SCALE, jnp.zeros((64,), F32)]))
    w['g2'] = head_row(jnp.concatenate([z64, qr_norm_g * MLA_SCALE, z32]))
    w['g3'] = head_row(jnp.concatenate([z64, _swap_halves(qr_norm_g) * MLA_SCALE, z32]))
    w['gk1'] = jnp.concatenate([kr_norm_g, z96])[None, :]
    w['gk3'] = jnp.concatenate([_swap_halves(kr_norm_g), z96])[None, :]
    w['wukp'] = jnp.pad(w_uk, ((0, 0), (0, 0), (0, LANES - MLA_NOPE))).reshape(
        MLA_KV_LORA, MLA_HEADS * LANES).astype(BF16)
    w['wuv'] = w_uv.reshape(MLA_KV_LORA, MLA_WIDTH).astype(BF16)
    pairs = w_uv.reshape(MLA_KV_LORA, MLA_HEADS // 2, 2, MLA_V)
    zero_v = jnp.zeros_like(pairs[:, :, 0])
    keep_even = jnp.stack([pairs[:, :, 0], zero_v], axis=2)
    keep_odd = jnp.stack([zero_v, pairs[:, :, 1]], axis=2)
    w['wuv2'] = jnp.stack([keep_even, keep_odd], axis=2).reshape(MLA_KV_LORA, 2 * MLA_WIDTH).astype(BF16)
    w['wukt'] = w_uk.reshape(MLA_KV_LORA, MLA_HEADS * MLA_NOPE).T.astype(BF16)
    wabs_nope = jnp.pad(jnp.transpose(w_uk, (1, 2, 0)), ((0, 0), (0, LANES - MLA_NOPE), (0, 0)))
    e_r = np.zeros((MLA_HEADS, LANES, LANES), np.float32)
    for j in range(MLA_ROPE):
        e_r[:, MLA_NOPE + j, j] = 1.0
    w['wabs'] = jnp.concatenate([wabs_nope, jnp.asarray(e_r)], axis=-1).reshape(
        MLA_HEADS * LANES, 2 * LANES).astype(BF16)
    w['gsb'] = sb_out_norm_g[None, :]
    w['gmla'] = mla_out_norm_g[None, :]
    w['wo'] = w_o.astype(BF16)
    w['n2g'] = norm2_g[None, :]
    wr_all = jnp.pad(jnp.concatenate([w_router_group, w_router_expert], axis=1),
                     ((0, 0), (0, LANES - N_GROUPS - N_EXPERTS)))
    w['wrh'], w['wrl'] = _split2(wr_all)
    w['br'] = jnp.pad(jnp.concatenate([b_router_group, b_router_expert]),
                      (0, LANES - N_GROUPS - N_EXPERTS))[None, :]
    return w


def _rope_tables(pos):
    half = MLA_ROPE // 2
    inv_freq = ROPE_THETA ** (-jnp.arange(half, dtype=F32) / half)
    ang = pos.astype(F32)[:, None] * inv_freq[None, :]
    z = jnp.zeros((pos.shape[0], 32), F32)

    def slab(v):
        return jnp.concatenate([v, v, z, v, v, z], axis=1)

    return slab(jnp.cos(ang)), slab(jnp.sin(ang))


def _pad_rows(a, rows):
    return jnp.pad(a, ((0, rows - a.shape[0]), (0, 0)))


def _kv_out(a, batch, length, tail):
    return a.reshape((1, batch, length) + tail)


def kernel(x_prompt, x_sample, cache_sb_k, cache_sb_v, cache_mla_ckv, cache_mla_krope, page_table, meta_tokens, norm1_g, w_in, cq_norm_g, ckv_norm_g, w_uq, qn_norm_g, qr_norm_g, kr_norm_g, w_uk, kn_norm_g, w_uv, sb_out_norm_g, mla_out_norm_g, w_o, norm2_g, w_router_group, b_router_group, w_router_expert, b_router_expert, w_gate, w_up, w_down):
    batch, seq, _ = x_prompt.shape
    nb, nq, _ = x_sample.shape
    n_past = page_table.shape[1] * cache_sb_k.shape[2]
    assert nq == 4 and seq % MLA_BLK == 0 and page_table.shape[1] % CHUNK_PAGES == 0
    w = _prep_weights(norm1_g[0], w_in[0], cq_norm_g[0], ckv_norm_g[0], w_uq[0], qn_norm_g[0],
                      qr_norm_g[0], kr_norm_g[0], w_uk[0], kn_norm_g[0], w_uv[0], sb_out_norm_g[0],
                      mla_out_norm_g[0], w_o[0], norm2_g[0], w_router_group[0], b_router_group[0],
                      w_router_expert[0], b_router_expert[0])
    wg, wu, wd = w_gate[0], w_up[0], w_down[0]

    cos_m, sin_m = _rope_tables(jnp.arange(N_META))
    cos_p, sin_p = _rope_tables(N_META + jnp.arange(seq))
    cos_s, sin_s = _rope_tables(jnp.tile(n_past + jnp.arange(nq), nb))
    xp2 = x_prompt.reshape(batch * seq, D_MODEL)
    xs2 = x_sample.reshape(nb * nq, D_MODEL)
    tm_s = min(ROW_TILE, nb * nq)
    pm = _project(meta_tokens, cos_m, sin_m, w, N_META)
    pp = _project(xp2, cos_p, sin_p, w, ROW_TILE)
    ps = _project(xs2, cos_s, sin_s, w, tm_s)

    sbo_p = _sb_prompt(pp['qsb'], pp['ksb16'], pp['vsb16'], _pad_rows(pm['ksb16'], BLK),
                       _pad_rows(pm['vsb16'], BLK), w['u2'], batch, seq)
    mlao_p = _mla_prompt(pp['qmla'], pp['kmla'], pp['vmla2'], _pad_rows(pm['kmla'], BLK),
                         _pad_rows(pm['vmla2'], BLK), batch, seq)

    kT = jnp.transpose(cache_sb_k, (0, 1, 3, 4, 2))
    vT = jnp.transpose(cache_sb_v, (0, 1, 3, 4, 2))
    krT = jnp.transpose(cache_mla_krope, (0, 1, 3, 2))
    qsb_g = ps['qsb'].reshape(nb, nq, SB_KV_HEADS, 2, SB_HEAD_DIM).transpose(0, 2, 3, 1, 4).reshape(
        nb, SB_KV_HEADS, 8, SB_HEAD_DIM)
    pad_keys = lambda a: jnp.pad(a.reshape(nb, nq, SB_KV_HEADS, SB_HEAD_DIM).transpose(0, 2, 1, 3),
                                 ((0, 0), (0, 0), (0, 8 - nq), (0, 0)))
    ksbn_g = pad_keys(ps['ksb'])
    vsbn_g = pad_keys(ps['vsb'])
    q_rows = jnp.tile(ps['qmla'].reshape(nb, nq, MLA_HEADS * LANES), (1, MLA_HEADS, 1))
    slab_of_lane = jnp.arange(MLA_HEADS * LANES)[None, :] // LANES
    head_of_row = jnp.arange(MLA_HEADS * nq)[:, None] // nq
    qbd = jnp.where((slab_of_lane == head_of_row)[None], q_rows, jnp.zeros((), BF16))
    kmn = jnp.pad(ps['kmla'].reshape(nb, nq, -1), ((0, 0), (0, 16 - nq), (0, 0)))
    vmn = jnp.pad(ps['vmla'].reshape(nb, nq, -1), ((0, 0), (0, 16 - nq), (0, 0)))
    sbo_g, mlao_full = _sample_attention(page_table, kT, vT, cache_mla_ckv, krT, qsb_g, ksbn_g, vsbn_g,
                                         qbd, kmn, vmn, w)
    sbo_s = sbo_g.reshape(nb, SB_KV_HEADS, 2, nq, SB_HEAD_DIM).transpose(0, 3, 1, 2, 4).reshape(
        nb * nq, SB_WIDTH)
    mf = mlao_full.reshape(nb, MLA_HEADS, nq, MLA_HEADS, MLA_V)
    mlao_s = jnp.stack([mf[:, h, :, h, :] for h in range(MLA_HEADS)], axis=2).reshape(nb * nq, MLA_WIDTH)

    def channel_mix(sbo, mlao, x2d, tm):
        h, t, ids, wts, counts = _post_attention(sbo, mlao, x2d, w, tm)
        return _moe(t, h, ids, wts, counts, wg, wu, wd, tm)

    y_prompt = channel_mix(sbo_p, mlao_p, xp2, ROW_TILE).reshape(batch, seq, D_MODEL)
    y_sample = channel_mix(sbo_s, mlao_s, xs2, tm_s).reshape(nb, nq, D_MODEL)

    def with_meta(m, p, tail):
        width = int(np.prod(tail))
        mb = jnp.broadcast_to(m[None, :, :width], (batch, N_META, width))
        full = jnp.concatenate([mb, p[:, :width].reshape(batch, seq, width)], axis=1)
        return full.reshape((1, batch, seq + N_META) + tail)

    kv_tail = (SB_KV_HEADS, SB_HEAD_DIM)
    return (y_prompt, y_sample,
            with_meta(pm['ksb'], pp['ksb'], kv_tail), with_meta(pm['vsb'], pp['vsb'], kv_tail),
            with_meta(pm['ckv'], pp['ckv'], (MLA_KV_LORA,)), with_meta(pm['kr'], pp['kr'], (MLA_ROPE,)),
            ps['ksb'].reshape((1, nb, nq) + kv_tail), ps['vsb'].reshape((1, nb, nq) + kv_tail),
            ps['ckv'].reshape(1, nb, nq, MLA_KV_LORA), ps['kr'][:, :MLA_ROPE].reshape(1, nb, nq, MLA_ROPE))
```

```python
import functools

import numpy as np
import jax
import jax.numpy as jnp
from jax import lax
from jax.experimental import pallas as pl
from jax.experimental.pallas import tpu as pltpu

F32 = jnp.float32
BF16 = jnp.bfloat16

D_MODEL = 1024
N_META = 16
EPS = 1e-6
SB_HEADS = 8
SB_KV_HEADS = 4
SB_HEAD_DIM = 64
SB_WIDTH = SB_HEADS * SB_HEAD_DIM
SB_SCALE = SB_HEAD_DIM ** -0.5
MLA_HEADS = 8
MLA_Q_LORA = 256
MLA_KV_LORA = 128
MLA_NOPE = 64
MLA_ROPE = 32
MLA_V = 64
MLA_WIDTH = MLA_HEADS * MLA_V
MLA_SCALE = (MLA_NOPE + MLA_ROPE) ** -0.5
ROPE_THETA = 10000.0
N_GROUPS = 4
EXPERTS_PER_GROUP = 8
N_EXPERTS = N_GROUPS * EXPERTS_PER_GROUP
D_EXPERT = 256

LANES = 128
BLK = 128
MLA_BLK = 256
MLA_TQ = 512
PAGE = 128
CHUNK_PAGES = 32
SUB_PAGES = 8
ROW_TILE = 256
EXPERT_TILE = 256
VMEM_LIMIT = 56 * 1024 * 1024
SB_DEAD_LOG = -104.0
NEG_BIG = -1e30

_IN_COLS = 512 + 256 + 256 + 256 + 128 + 128 + 128


def _full(shape):
    nd = len(shape)
    return pl.BlockSpec(shape, lambda *_: (0,) * nd)


def _split2(x):
    hi = x.astype(BF16)
    lo = (x - hi.astype(F32)).astype(BF16)
    return hi, lo


def _dot(a, b):
    return jnp.dot(a, b, preferred_element_type=F32)


def _dot_nt(a, b):
    return lax.dot_general(a, b, (((1,), (1,)), ((), ())), preferred_element_type=F32)


def _softplus(z):
    return jnp.maximum(z, 0.0) + jnp.log1p(jnp.exp(-jnp.abs(z)))


def _proj_body(x_ref, cos_ref, sin_ref, n1g_ref, win_ref, cqg_ref, ckvg_ref, wa_ref, wb_ref,
               sq_ref, sqt_ref, g1_ref, g2_ref, g3_ref, gk1_ref, gk3_ref, wukp_ref, sk_ref,
               skt_ref, wuv_ref, wuv2_ref,
               qsb_ref, ksb_ref, vsb_ref, ksb16_ref, vsb16_ref, ckv_ref, kr_ref, qmla_ref,
               kmla_ref, vmla_ref, vmla2_ref):
    x = x_ref[...]
    xn = x * lax.rsqrt(jnp.mean(x * x, axis=-1, keepdims=True) + EPS) * n1g_ref[...]
    p = _dot(xn.astype(BF16), win_ref[...])
    qsb_ref[...] = p[:, 0:512]
    ksb = p[:, 512:768]
    vsb = p[:, 768:1024]
    ksb_ref[...] = ksb
    vsb_ref[...] = vsb
    ksb16_ref[...] = ksb.astype(BF16)
    vsb16_ref[...] = vsb.astype(BF16)

    cq = p[:, 1024:1280]
    cq = cq * lax.rsqrt(jnp.mean(cq * cq, axis=-1, keepdims=True) + EPS) * cqg_ref[...]
    ckv = p[:, 1280:1408]
    ckv = ckv * lax.rsqrt(jnp.mean(ckv * ckv, axis=-1, keepdims=True) + EPS) * ckvg_ref[...]
    ckv_ref[...] = ckv

    cos = cos_ref[...]
    sin = sin_ref[...]
    s1 = p[:, 1408:1536]
    s2 = p[:, 1536:1664]
    inv_kr = lax.rsqrt(jnp.sum(s1 * s1, axis=-1, keepdims=True) * (1.0 / MLA_ROPE) + EPS)
    kr = (s1 * (cos * gk1_ref[...]) + s2 * (sin * gk3_ref[...])) * inv_kr
    kr_ref[...] = kr

    cqb = cq.astype(BF16)
    a = _dot(cqb, wa_ref[...])
    b = _dot(cqb, wb_ref[...])
    hi, lo = _split2(a * a)
    msq = _dot(hi, sq_ref[...]) + _dot(lo, sq_ref[...])
    ihi, ilo = _split2(lax.rsqrt(msq + EPS))
    invf = _dot(ihi, sqt_ref[...]) + _dot(ilo, sqt_ref[...])

    ckvb = ckv.astype(BF16)
    kraw = _dot(ckvb, wukp_ref[...])
    khi, klo = _split2(kraw * kraw)
    kmsq = _dot(khi, sk_ref[...]) + _dot(klo, sk_ref[...])
    kihi, kilo = _split2(lax.rsqrt(kmsq + EPS))
    kinvf = _dot(kihi, skt_ref[...]) + _dot(kilo, skt_ref[...])
    kr_shift = pltpu.roll(kr, 64, axis=1)

    for h in range(MLA_HEADS):
        sl = slice(h * LANES, (h + 1) * LANES)
        qh = (a[:, sl] * (g1_ref[:, sl] + cos * g2_ref[:, sl])
              + b[:, sl] * (sin * g3_ref[:, sl])) * invf[:, sl]
        qmla_ref[:, sl] = qh.astype(BF16)
        kmla_ref[:, sl] = (kraw[:, sl] * kinvf[:, sl] + kr_shift).astype(BF16)
    vmla_ref[...] = _dot(ckvb, wuv_ref[...]).astype(BF16)
    vmla2_ref[...] = _dot(ckvb, wuv2_ref[...]).astype(BF16)


def _project(x2d, cos_t, sin_t, w, tm):
    t = x2d.shape[0]
    tbl_blocks = cos_t.shape[0] // tm
    row = lambda width: pl.BlockSpec((tm, width), lambda i: (i, 0))
    tbl = pl.BlockSpec((tm, LANES), lambda i: (i % tbl_blocks, 0))
    consts = [w['n1g'], w['win'], w['cqg'], w['ckvg'], w['wa'], w['wb'], w['sq'], w['sqt'],
              w['g1'], w['g2'], w['g3'], w['gk1'], w['gk3'], w['wukp'], w['sk'], w['skt'], w['wuv'], w['wuv2']]
    out_widths = [(512, F32), (256, F32), (256, F32), (256, BF16), (256, BF16), (128, F32),
                  (128, F32), (1024, BF16), (1024, BF16), (512, BF16), (1024, BF16)]
    outs = pl.pallas_call(
        _proj_body,
        grid=(t // tm,),
        in_specs=[row(D_MODEL), tbl, tbl] + [_full(c.shape) for c in consts],
        out_specs=[row(wd) for wd, _ in out_widths],
        out_shape=[jax.ShapeDtypeStruct((t, wd), dt) for wd, dt in out_widths],
        compiler_params=pltpu.CompilerParams(dimension_semantics=("arbitrary",),
                                             vmem_limit_bytes=VMEM_LIMIT),
        name="proj",
    )(x2d, cos_t, sin_t, *consts)
    names = ['qsb', 'ksb', 'vsb', 'ksb16', 'vsb16', 'ckv', 'kr', 'qmla', 'kmla', 'vmla', 'vmla2']
    return dict(zip(names, outs))


def _sb_weights(z, vis, u, carry):
    n = z.shape[1]
    sp = _softplus(z)
    log_rest = -sp
    if vis is not None:
        log_rest = jnp.where(vis, log_rest, 0.0)
    hi, lo = _split2(log_rest)
    cs = _dot(hi, u) + _dot(lo, u)
    wgt = jnp.exp(z - sp + cs[:, :n] + carry)
    if vis is not None:
        wgt = jnp.where(vis, wgt, 0.0)
    return wgt.astype(BF16), cs[:, n:]


def _sb_block(q2, kj, vj, vis, u2, carry_ref, acc_ref, k_is_transposed=False):
    z = _dot(q2, kj) if k_is_transposed else _dot_nt(q2, kj)
    c = carry_ref[...]
    wb, total = _sb_weights(z, vis, u2, c)
    acc_ref[...] += _dot_nt(wb, vj) if k_is_transposed else _dot(wb, vj)
    carry_ref[...] = c + total


def _sb_prompt_body(q_ref, k_ref, v_ref, km_ref, vm_ref, u2_ref, o_ref, q2_ref, carry_ref, acc_ref):
    i = pl.program_id(1)
    lane = lax.broadcasted_iota(jnp.int32, (1, LANES), 1)
    lane_lo = lane < 64
    n_rows = SB_HEADS * BLK
    half = n_rows // 2
    row_t = lax.broadcasted_iota(jnp.int32, (n_rows, 1), 0) % BLK
    vis_diag = lane < row_t
    vis_meta = jnp.broadcast_to(lane < N_META, (n_rows, LANES))
    u2 = u2_ref[...]
    for g in range(SB_KV_HEADS):
        slab = q_ref[:, g * LANES:(g + 1) * LANES]
        rolled = pltpu.roll(slab, 64, axis=1)
        if g % 2 == 0:
            q_r0 = jnp.where(lane_lo, slab, 0.0)
            q_r1 = jnp.where(lane_lo, rolled, 0.0)
        else:
            q_r0 = jnp.where(lane_lo, 0.0, rolled)
            q_r1 = jnp.where(lane_lo, 0.0, slab)
        q2_ref[(2 * g) * BLK:(2 * g + 1) * BLK, :] = q_r0.astype(BF16)
        q2_ref[(2 * g + 1) * BLK:(2 * g + 2) * BLK, :] = q_r1.astype(BF16)
    carry_ref[...] = jnp.zeros_like(carry_ref)
    acc_ref[...] = jnp.zeros_like(acc_ref)

    def block(k_of_pair, v_of_pair, vis):
        z = jnp.concatenate([_dot_nt(q2_ref[0:half, :], k_of_pair(0)),
                             _dot_nt(q2_ref[half:, :], k_of_pair(1))], axis=0)
        c = carry_ref[...]
        wb, total = _sb_weights(z, vis, u2, c)
        acc_ref[0:half, :] += _dot(wb[0:half], v_of_pair(0))
        acc_ref[half:, :] += _dot(wb[half:], v_of_pair(1))
        carry_ref[...] = c + total

    def run(j, vis):
        rows = pl.ds(pl.multiple_of(j * BLK, BLK), BLK)
        block(lambda p: k_ref[rows, p * LANES:(p + 1) * LANES],
              lambda p: v_ref[rows, p * LANES:(p + 1) * LANES], vis)

    run(i, vis_diag)

    def cond(st):
        j, alive = st
        return jnp.logical_and(j >= 0, alive > SB_DEAD_LOG)

    def body(st):
        j, _ = st
        run(j, None)
        return j - 1, jnp.max(carry_ref[...])

    _, alive = lax.while_loop(cond, body, (i - 1, jnp.max(carry_ref[...])))

    @pl.when(alive > SB_DEAD_LOG)
    def _():
        block(lambda p: km_ref[:, p * LANES:(p + 1) * LANES],
              lambda p: vm_ref[:, p * LANES:(p + 1) * LANES], vis_meta)

    for g in range(SB_KV_HEADS):
        o_r0 = acc_ref[(2 * g) * BLK:(2 * g + 1) * BLK, :]
        o_r1 = acc_ref[(2 * g + 1) * BLK:(2 * g + 2) * BLK, :]
        if g % 2 == 0:
            out = jnp.where(lane_lo, o_r0, pltpu.roll(o_r1, 64, axis=1))
        else:
            out = jnp.where(lane_lo, pltpu.roll(o_r0, 64, axis=1), o_r1)
        o_ref[:, g * LANES:(g + 1) * LANES] = out


def _sb_prompt(qsb, ksb16, vsb16, km16, vm16, u2, batch, seq):
    nq = seq // BLK
    kv_w = SB_KV_HEADS * SB_HEAD_DIM
    return pl.pallas_call(
        _sb_prompt_body,
        grid=(batch, nq),
        in_specs=[pl.BlockSpec((BLK, SB_WIDTH), lambda b, i: (b * nq + i, 0)),
                  pl.BlockSpec((seq, kv_w), lambda b, i: (b, 0)),
                  pl.BlockSpec((seq, kv_w), lambda b, i: (b, 0)),
                  _full(km16.shape), _full(vm16.shape), _full(u2.shape)],
        out_specs=pl.BlockSpec((BLK, SB_WIDTH), lambda b, i: (b * nq + i, 0)),
        out_shape=jax.ShapeDtypeStruct((batch * seq, SB_WIDTH), F32),
        scratch_shapes=[pltpu.VMEM((SB_HEADS * BLK, LANES), BF16), pltpu.VMEM((SB_HEADS * BLK, LANES), F32),
                        pltpu.VMEM((SB_HEADS * BLK, LANES), F32)],
        compiler_params=pltpu.CompilerParams(dimension_semantics=("arbitrary", "arbitrary"),
                                             vmem_limit_bytes=VMEM_LIMIT),
        name="sb_prompt",
    )(qsb, ksb16, vsb16, km16, vm16, u2)


def _mla_prompt_body(q_ref, k_ref, v2_ref, km_ref, vm2_ref, ones_ref, o_ref, m_ref, l_ref, acc_ref):
    i = pl.program_id(1)
    tq = MLA_TQ
    tk = MLA_BLK
    per_q = tq // tk
    lane = lax.broadcasted_iota(jnp.int32, (1, LANES), 1)
    lane_lo = lane < 64
    col = lax.broadcasted_iota(jnp.int32, (1, tk), 1)
    row_t = lax.broadcasted_iota(jnp.int32, (tq, 1), 0)
    vis_meta = jnp.broadcast_to(lane < N_META, (tq, LANES))
    m_ref[...] = jnp.full_like(m_ref, NEG_BIG)
    l_ref[...] = jnp.zeros_like(l_ref)
    acc_ref[...] = jnp.zeros_like(acc_ref)

    def block(load_k, load_v2, vis):
        for hp in range(MLA_HEADS // 2):
            ps, alphas = [], []
            for par in range(2):
                h = 2 * hp + par
                s = _dot_nt(q_ref[:, h * LANES:(h + 1) * LANES], load_k(h))
                if vis is not None:
                    s = jnp.where(vis, s, NEG_BIG)
                m_old = m_ref[h]
                m_new = jnp.maximum(m_old, jnp.max(s, axis=-1, keepdims=True))
                p = jnp.exp(s - jnp.tile(m_new, (1, s.shape[1] // LANES)))
                alpha = jnp.exp(m_old - m_new)
                m_ref[h] = m_new
                ps.append(p.astype(BF16))
                alphas.append(alpha)
            pcat = jnp.concatenate(ps, axis=1)
            vbd = jnp.concatenate([load_v2(2 * hp), load_v2(2 * hp + 1)], axis=0)
            scale = jnp.where(lane_lo, alphas[0], alphas[1])
            nk = pcat.shape[1] // 2
            ones = ones_ref[MLA_BLK - nk:MLA_BLK + nk, :]
            pv = _dot(pcat, jnp.concatenate([vbd, ones], axis=1))
            acc_ref[hp] = acc_ref[hp] * scale + pv[:, :LANES]
            l_ref[hp] = l_ref[hp] * scale + pv[:, LANES:]

    block(lambda h: km_ref[:, h * LANES:(h + 1) * LANES],
          lambda c: vm2_ref[:, c * LANES:(c + 1) * LANES], vis_meta)

    def key_block(j, vis):
        rows = pl.ds(pl.multiple_of(j * tk, tk), tk)
        block(lambda h: k_ref[rows, h * LANES:(h + 1) * LANES],
              lambda c: v2_ref[rows, c * LANES:(c + 1) * LANES], vis)

    def body(j, carry):
        key_block(j, None)
        return carry

    lax.fori_loop(0, i * per_q, body, 0)
    for d in range(per_q):
        key_block(i * per_q + d, col + d * tk <= row_t)
    for hp in range(MLA_HEADS // 2):
        o_ref[:, hp * LANES:(hp + 1) * LANES] = acc_ref[hp] / l_ref[hp]


def _mla_prompt(qmla, kmla, vmla2, kmm, vmm2, batch, seq):
    nq = seq // MLA_TQ
    width = MLA_HEADS * LANES
    ones_np = np.zeros((2 * MLA_BLK, LANES), np.float32)
    ones_np[:MLA_BLK, :64] = 1.0
    ones_np[MLA_BLK:, 64:] = 1.0
    ones_bd = jnp.asarray(ones_np, dtype=BF16)
    return pl.pallas_call(
        _mla_prompt_body,
        grid=(batch, nq),
        in_specs=[pl.BlockSpec((MLA_TQ, width), lambda b, i: (b * nq + i, 0)),
                  pl.BlockSpec((seq, width), lambda b, i: (b, 0)),
                  pl.BlockSpec((seq, width), lambda b, i: (b, 0)),
                  _full(kmm.shape), _full(vmm2.shape), _full(ones_bd.shape)],
        out_specs=pl.BlockSpec((MLA_TQ, MLA_WIDTH), lambda b, i: (b * nq + i, 0)),
        out_shape=jax.ShapeDtypeStruct((batch * seq, MLA_WIDTH), F32),
        scratch_shapes=[pltpu.VMEM((MLA_HEADS, MLA_TQ, LANES), F32),
                        pltpu.VMEM((MLA_HEADS // 2, MLA_TQ, LANES), F32),
                        pltpu.VMEM((MLA_HEADS // 2, MLA_TQ, LANES), F32)],
        compiler_params=pltpu.CompilerParams(dimension_semantics=("arbitrary", "arbitrary"),
                                             vmem_limit_bytes=VMEM_LIMIT),
        name="mla_prompt",
    )(qmla, kmla, vmla2, kmm, vmm2, ones_bd)


def _sample_body(pt_ref,
                 kT_hbm, vT_hbm, ckv_hbm, krT_hbm,
                 qsb_ref, ksbn_ref, vsbn_ref, qbd_ref, kmn_ref, vmn_ref,
                 wukt_ref, wabs_ref, shs_ref, wuv_ref, u2_ref, u3_ref,
                 sbo_ref, mlao_ref,
                 cbuf, rbuf, sbk, sbv, wq_ref, qr_ref, m_ref, l_ref, lat_ref,
                 carry_ref, sbacc_ref, sem_c, sem_r, sem_sb):
    b = pl.program_id(0)
    c = pl.program_id(1)
    nb = pl.num_programs(0)
    nc = pl.num_programs(1)
    n_pages = nc * CHUNK_PAGES
    step = b * nc + c
    slot = step % 2
    chunk = CHUNK_PAGES * PAGE
    rows_q = MLA_HEADS * 4

    def chunk_copies(bb, cc, sl):
        cps = []
        for pg in range(CHUNK_PAGES):
            page = pt_ref[bb, cc * CHUNK_PAGES + pg]
            cps.append(pltpu.make_async_copy(
                ckv_hbm.at[0, page], cbuf.at[sl, pl.ds(pg * PAGE, PAGE), :], sem_c.at[sl]))
            cps.append(pltpu.make_async_copy(
                krT_hbm.at[0, page], rbuf.at[sl, :, pl.ds(pg * PAGE, PAGE)], sem_r.at[sl]))
        return cps

    def sb_copies(bb, page_idx, sl):
        page = pt_ref[bb, page_idx]
        return [pltpu.make_async_copy(kT_hbm.at[0, page], sbk.at[sl], sem_sb.at[sl, 0]),
                pltpu.make_async_copy(vT_hbm.at[0, page], sbv.at[sl], sem_sb.at[sl, 1])]

    @pl.when(step == 0)
    def _():
        for cp in chunk_copies(0, 0, 0):
            cp.start()

    @pl.when(c == 0)
    def _():
        for sl in range(2):
            for cp in sb_copies(b, n_pages - 1 - sl, sl):
                cp.start()
        qa = _dot(qbd_ref[0], wabs_ref[...])
        wq_ref[0:512, :] = wukt_ref[...]
        wq_ref[512:512 + rows_q, :] = qa[:, :LANES].astype(BF16)
        qr_ref[...] = qa[:, LANES:LANES + MLA_ROPE].astype(BF16)
        m_ref[...] = jnp.full_like(m_ref, NEG_BIG)
        l_ref[...] = jnp.zeros_like(l_ref)
        lat_ref[...] = jnp.zeros_like(lat_ref)

    pltpu.make_async_copy(cbuf.at[slot], cbuf.at[slot], sem_c.at[slot]).wait()
    pltpu.make_async_copy(rbuf.at[slot], rbuf.at[slot], sem_r.at[slot]).wait()

    @pl.when(step + 1 < nb * nc)
    def _():
        nxt = step + 1
        for cp in chunk_copies(nxt // nc, nxt % nc, 1 - slot):
            cp.start()

    sub = SUB_PAGES * PAGE
    cbs, scores = [], []
    for u in range(CHUNK_PAGES // SUB_PAGES):
        cols = slice(u * sub, (u + 1) * sub)
        cb = cbuf[slot, cols, :].astype(BF16)
        big = _dot_nt(wq_ref[...], cb)
        k2 = big[0:512] * big[0:512]
        part = k2.reshape(MLA_HEADS, MLA_NOPE // 8, 8, sub).sum(axis=1).reshape(MLA_HEADS * 8, sub)
        ksum = _dot(shs_ref[...], part.astype(BF16))
        scores.append(big[512:512 + rows_q] * lax.rsqrt(ksum + EPS)
                      + _dot(qr_ref[...], rbuf[slot, :, cols].astype(BF16)))
        cbs.append(cb)
    s = jnp.concatenate(scores, axis=1)
    m_old = m_ref[...]
    m_new = jnp.maximum(m_old, jnp.max(s, axis=-1, keepdims=True))
    p = jnp.exp(s - m_new).astype(BF16)
    alpha = jnp.exp(m_old - m_new)
    l_ref[...] = alpha * l_ref[...] + jnp.sum(p.astype(F32), axis=-1, keepdims=True)
    pv = _dot(p[:, 0:sub], cbs[0])
    for u in range(1, len(cbs)):
        pv += _dot(p[:, u * sub:(u + 1) * sub], cbs[u])
    lat_ref[...] = alpha * lat_ref[...] + pv
    m_ref[...] = m_new

    @pl.when(c == nc - 1)
    def _():
        lane8 = lax.broadcasted_iota(jnp.int32, (rows_q, 16), 1)
        q_of_row = lax.broadcasted_iota(jnp.int32, (rows_q, 16), 0) % 4
        s_new = _dot_nt(qbd_ref[0], kmn_ref[0])
        s_new = jnp.where(lane8 <= q_of_row, s_new, NEG_BIG)
        m_old = m_ref[...]
        m_fin = jnp.maximum(m_old, jnp.max(s_new, axis=-1, keepdims=True))
        p_new = jnp.exp(s_new - m_fin)
        alpha = jnp.exp(m_old - m_fin)
        l_fin = alpha * l_ref[...] + jnp.sum(p_new, axis=-1, keepdims=True)
        o = _dot((alpha * lat_ref[...]).astype(BF16), wuv_ref[...]) + _dot(p_new.astype(BF16), vmn_ref[0])
        mlao_ref[0] = o / l_fin

        u2 = u2_ref[...]
        n_first = 3 * PAGE
        lane3 = lax.broadcasted_iota(jnp.int32, (1, n_first), 1)
        q_row = lax.broadcasted_iota(jnp.int32, (SB_KV_HEADS * 8, 1), 0) % 4
        vis_first = lane3 < 2 * PAGE + q_row
        zpad = jnp.zeros((PAGE - 8, SB_HEAD_DIM), F32)
        for sl in range(2):
            for cp in sb_copies(b, n_pages - 1 - sl, sl):
                cp.wait()
        vns, zs = [], []
        for g in range(SB_KV_HEADS):
            qg = qsb_ref[0, g].astype(BF16)
            kn = jnp.concatenate([ksbn_ref[0, g], zpad], axis=0).astype(BF16)
            vn = jnp.concatenate([vsbn_ref[0, g], zpad], axis=0).astype(BF16)
            zs.append(jnp.concatenate([_dot(qg, sbk[1, g].astype(BF16)), _dot(qg, sbk[0, g].astype(BF16)),
                                       _dot_nt(qg, kn)], axis=1))
            vns.append(vn)
        z = jnp.concatenate(zs, axis=0)
        wb, total = _sb_weights(z, vis_first, u3_ref[...], 0.0)
        carry_ref[...] = total
        for g in range(SB_KV_HEADS):
            wg = wb[g * 8:(g + 1) * 8]
            sbacc_ref[g * 8:(g + 1) * 8, :] = (
                _dot_nt(wg[:, 0:PAGE], sbv[1, g].astype(BF16))
                + _dot_nt(wg[:, PAGE:2 * PAGE], sbv[0, g].astype(BF16))
                + _dot(wg[:, 2 * PAGE:], vns[g]))

        def sb_page(sl):
            for g in range(SB_KV_HEADS):
                qg = qsb_ref[0, g].astype(BF16)
                rows = slice(g * 8, (g + 1) * 8)
                _sb_block(qg, sbk[sl, g].astype(BF16), sbv[sl, g].astype(BF16), None, u2,
                          carry_ref.at[rows], sbacc_ref.at[rows], k_is_transposed=True)

        def cond(st):
            j, alive = st
            return jnp.logical_and(j >= 0, alive > SB_DEAD_LOG)

        def body(st):
            j, _ = st
            cps = sb_copies(b, j, 0)
            for cp in cps:
                cp.start()
            for cp in cps:
                cp.wait()
            sb_page(0)
            return j - 1, jnp.max(carry_ref[...])

        lax.while_loop(cond, body, (n_pages - 3, jnp.max(carry_ref[...])))
        for g in range(SB_KV_HEADS):
            sbo_ref[0, g] = sbacc_ref[g * 8:(g + 1) * 8, :]


def _sample_attention(page_table, kT, vT, ckv_cache, krT, qsb_g, ksbn_g, vsbn_g, qbd, kmn, vmn, w):
    nb, n_pages = page_table.shape
    nc = n_pages // CHUNK_PAGES
    chunk = CHUNK_PAGES * PAGE
    rows_q = MLA_HEADS * 4
    per_b = lambda shape: pl.BlockSpec((1,) + shape, lambda b, c, pt: (b,) + (0,) * len(shape))
    const = lambda a: pl.BlockSpec(a.shape, lambda b, c, pt: (0,) * a.ndim)
    consts = [w['wukt'], w['wabs'], w['shs'], w['wuv'], w['u2'], w['u3']]
    any_spec = pl.BlockSpec(memory_space=pl.ANY)
    return pl.pallas_call(
        _sample_body,
        grid_spec=pltpu.PrefetchScalarGridSpec(
            num_scalar_prefetch=1,
            grid=(nb, nc),
            in_specs=[any_spec, any_spec, any_spec, any_spec,
                      per_b((SB_KV_HEADS, 8, SB_HEAD_DIM)), per_b((SB_KV_HEADS, 8, SB_HEAD_DIM)),
                      per_b((SB_KV_HEADS, 8, SB_HEAD_DIM)), per_b((rows_q, MLA_HEADS * LANES)),
                      per_b((16, MLA_HEADS * LANES)), per_b((16, MLA_WIDTH))]
                     + [const(a) for a in consts],
            out_specs=[per_b((SB_KV_HEADS, 8, SB_HEAD_DIM)), per_b((rows_q, MLA_WIDTH))],
            scratch_shapes=[
                pltpu.VMEM((2, chunk, MLA_KV_LORA), F32),
                pltpu.VMEM((2, MLA_ROPE, chunk), F32),
                pltpu.VMEM((2, SB_KV_HEADS, SB_HEAD_DIM, PAGE), F32),
                pltpu.VMEM((2, SB_KV_HEADS, SB_HEAD_DIM, PAGE), F32),
                pltpu.VMEM((512 + rows_q, MLA_KV_LORA), BF16),
                pltpu.VMEM((rows_q, MLA_ROPE), BF16),
                pltpu.VMEM((rows_q, 1), F32),
                pltpu.VMEM((rows_q, 1), F32),
                pltpu.VMEM((rows_q, MLA_KV_LORA), F32),
                pltpu.VMEM((SB_KV_HEADS * 8, LANES), F32),
                pltpu.VMEM((SB_KV_HEADS * 8, SB_HEAD_DIM), F32),
                pltpu.SemaphoreType.DMA((2,)),
                pltpu.SemaphoreType.DMA((2,)),
                pltpu.SemaphoreType.DMA((2, 2)),
            ]),
        out_shape=[jax.ShapeDtypeStruct((nb, SB_KV_HEADS, 8, SB_HEAD_DIM), F32),
                   jax.ShapeDtypeStruct((nb, rows_q, MLA_WIDTH), F32)],
        compiler_params=pltpu.CompilerParams(dimension_semantics=("arbitrary", "arbitrary"),
                                             vmem_limit_bytes=VMEM_LIMIT),
        name="sample_attn",
    )(page_table, kT, vT, ckv_cache, krT, qsb_g, ksbn_g, vsbn_g, qbd, kmn, vmn, *consts)


def _post_body(sbo_ref, mlao_ref, x_ref, gsb_ref, gmla_ref, wo_ref, n2g_ref, wrh_ref, wrl_ref,
               br_ref, ltri_ref, h_ref, t_ref, ids_ref, wts_ref, counts_ref, cnt_ref):
    sbo = sbo_ref[...]
    mlao = mlao_ref[...]
    m_sb = sbo * lax.rsqrt(jnp.mean(sbo * sbo, axis=-1, keepdims=True) + EPS) * gsb_ref[...]
    m_mla = mlao * lax.rsqrt(jnp.mean(mlao * mlao, axis=-1, keepdims=True) + EPS) * gmla_ref[...]
    h = (x_ref[...] + _dot(m_sb.astype(BF16), wo_ref[0:SB_WIDTH, :])
         + _dot(m_mla.astype(BF16), wo_ref[SB_WIDTH:, :]))
    h_ref[...] = h
    t = h * lax.rsqrt(jnp.mean(h * h, axis=-1, keepdims=True) + EPS) * n2g_ref[...]
    t_ref[...] = t
    thi, tlo = _split2(t)
    lg = (_dot(thi, wrh_ref[...]) + _dot(tlo, wrh_ref[...]) + _dot(thi, wrl_ref[...])
          + br_ref[...])
    tm = lg.shape[0]
    lane_i = lax.broadcasted_iota(jnp.int32, (tm, LANES), 1)
    lane = lane_i.astype(F32)
    big_i = jnp.float32(1 << 20)
    is_grp = lane < N_GROUPS
    mg = jnp.max(jnp.where(is_grp, lg, -jnp.inf), axis=-1, keepdims=True)
    gidx = jnp.min(jnp.where(is_grp & (lg == mg), lane, big_i), axis=-1, keepdims=True)
    p_grp = 1.0 / jnp.sum(jnp.where(is_grp, jnp.exp(lg - mg), 0.0), axis=-1, keepdims=True)
    lo_lane = N_GROUPS + gidx * EXPERTS_PER_GROUP
    sel = (lane >= lo_lane) & (lane < lo_lane + EXPERTS_PER_GROUP)
    v1 = jnp.max(jnp.where(sel, lg, -jnp.inf), axis=-1, keepdims=True)
    i1 = jnp.min(jnp.where(sel & (lg == v1), lane, big_i), axis=-1, keepdims=True)
    sel2 = sel & (lane != i1)
    v2 = jnp.max(jnp.where(sel2, lg, -jnp.inf), axis=-1, keepdims=True)
    i2 = jnp.min(jnp.where(sel2 & (lg == v2), lane, big_i), axis=-1, keepdims=True)
    e21 = jnp.exp(v2 - v1)
    w1 = p_grp / (1.0 + e21)
    w2 = w1 * e21
    e1 = i1 - N_GROUPS
    e2 = i2 - N_GROUPS
    @pl.when(pl.program_id(0) == 0)
    def _():
        cnt_ref[...] = jnp.zeros_like(cnt_ref)

    oh1 = jnp.where(lane == e1, 1.0, 0.0)
    oh2 = jnp.where(lane == e2, 1.0, 0.0)
    both = oh1 + oh2
    before = _dot(ltri_ref[...], both.astype(BF16)) + cnt_ref[0:1, :]
    r1 = jnp.sum(oh1 * before, axis=-1, keepdims=True)
    r2 = jnp.sum(oh2 * before, axis=-1, keepdims=True)
    cnt_new = cnt_ref[...] + jnp.sum(both, axis=0, keepdims=True)
    cnt_ref[...] = cnt_new
    counts_ref[...] = cnt_new
    ids = jnp.where(lane_i == 0, e1, jnp.where(lane_i == 1, e2,
                    jnp.where(lane_i == 2, r1, jnp.where(lane_i == 3, r2, 0.0))))
    ids_ref[...] = ids.astype(jnp.int32)
    wts_ref[...] = jnp.where(lane_i == 0, w1, jnp.where(lane_i == 1, w2, 0.0))


def _post_attention(sbo, mlao, x2d, w, tm):
    t = x2d.shape[0]
    row = lambda width: pl.BlockSpec((tm, width), lambda i: (i, 0))
    ltri = jnp.asarray(np.tril(np.ones((tm, tm), np.float32), -1), dtype=BF16)
    consts = [w['gsb'], w['gmla'], w['wo'], w['n2g'], w['wrh'], w['wrl'], w['br'], ltri]
    return pl.pallas_call(
        _post_body,
        grid=(t // tm,),
        in_specs=[row(SB_WIDTH), row(MLA_WIDTH), row(D_MODEL)] + [_full(c.shape) for c in consts],
        out_specs=[row(D_MODEL), row(D_MODEL), row(LANES), row(LANES), _full((8, LANES))],
        out_shape=[jax.ShapeDtypeStruct((t, D_MODEL), F32), jax.ShapeDtypeStruct((t, D_MODEL), F32),
                   jax.ShapeDtypeStruct((t, LANES), jnp.int32), jax.ShapeDtypeStruct((t, LANES), F32),
                   jax.ShapeDtypeStruct((8, LANES), F32)],
        scratch_shapes=[pltpu.VMEM((8, LANES), F32)],
        compiler_params=pltpu.CompilerParams(dimension_semantics=("arbitrary",),
                                             vmem_limit_bytes=VMEM_LIMIT),
        name="post_attn",
    )(sbo, mlao, x2d, *consts)


def _wait_rows(src, dst, sem, cnt):
    def wait8(_, c):
        pltpu.make_async_copy(src.at[pl.ds(0, 8), :], dst.at[pl.ds(0, 8), :], sem).wait()
        return c

    def wait1(_, c):
        pltpu.make_async_copy(src.at[pl.ds(0, 1), :], dst.at[pl.ds(0, 1), :], sem).wait()
        return c

    lax.fori_loop(0, cnt // 8, wait8, 0)
    lax.fori_loop(0, cnt % 8, wait1, 0)


def _dispatch_body(pos_ref, tile_cnt_ref, npad_ref, t_ref, xs_hbm, zbuf, sem, sem_pad, sem_tile):
    i = pl.program_id(0)
    tm = t_ref.shape[0]

    def empty_tile_copy(n):
        return pltpu.make_async_copy(zbuf, xs_hbm.at[pl.ds(n * EXPERT_TILE, EXPERT_TILE), :], sem_tile)

    @pl.when(i == 0)
    def _():
        zbuf[...] = jnp.zeros_like(zbuf)

        def per_tile(n, c):
            cnt = tile_cnt_ref[n]

            @pl.when(cnt == 0)
            def _():
                empty_tile_copy(n).start()

            def per_row(r, c2):
                pltpu.make_async_copy(zbuf.at[pl.ds(0, 1), :],
                                      xs_hbm.at[pl.ds(n * EXPERT_TILE + r, 1), :], sem_pad).start()
                return c2

            lax.fori_loop(jnp.where(cnt > 0, cnt, EXPERT_TILE), EXPERT_TILE, per_row, 0)
            return c

        lax.fori_loop(0, tile_cnt_ref.shape[0], per_tile, 0)

    def send(r, c):
        a = 2 * (i * tm + r)
        for k in range(2):
            pltpu.make_async_copy(t_ref.at[pl.ds(r, 1), :],
                                  xs_hbm.at[pl.ds(pos_ref[a + k], 1), :], sem).start()
        return c

    lax.fori_loop(0, tm, send, 0, unroll=8)
    for _ in range(2):
        pltpu.make_async_copy(t_ref, xs_hbm.at[pl.ds(0, tm), :], sem).wait()

    @pl.when(i == pl.num_programs(0) - 1)
    def _():
        _wait_rows(zbuf, xs_hbm, sem_pad, npad_ref[0])

        def wait_tile(_, c):
            empty_tile_copy(0).wait()
            return c

        lax.fori_loop(0, npad_ref[1], wait_tile, 0)


def _dispatch(t2d, pos_flat, tile_cnt, n_pad, n_rows, tm):
    t = t2d.shape[0]
    return pl.pallas_call(
        _dispatch_body,
        grid_spec=pltpu.PrefetchScalarGridSpec(
            num_scalar_prefetch=3,
            grid=(t // tm,),
            in_specs=[pl.BlockSpec((tm, D_MODEL), lambda i, *_: (i, 0))],
            out_specs=pl.BlockSpec(memory_space=pl.ANY),
            scratch_shapes=[pltpu.VMEM((EXPERT_TILE, D_MODEL), F32), pltpu.SemaphoreType.DMA(()),
                            pltpu.SemaphoreType.DMA(()), pltpu.SemaphoreType.DMA(())]),
        out_shape=jax.ShapeDtypeStruct((n_rows, D_MODEL), F32),
        compiler_params=pltpu.CompilerParams(dimension_semantics=("arbitrary",),
                                             vmem_limit_bytes=VMEM_LIMIT),
        name="moe_dispatch",
    )(pos_flat, tile_cnt, n_pad, t2d)


def _moe_body(tile_e_ref, tile_src_ref, tile_cnt_ref, x_ref, wg_ref, wu_ref, wd_ref, y_ref):
    cnt = tile_cnt_ref[pl.program_id(0)]

    @pl.when(cnt == 0)
    def _():
        y_ref[...] = jnp.zeros_like(y_ref)

    @pl.when(cnt > 0)
    def _():
        xb = x_ref[...].astype(BF16)
        gate = _dot(xb, wg_ref[0].astype(BF16))
        up = _dot(xb, wu_ref[0].astype(BF16))
        hid = gate * jax.nn.sigmoid(gate) * up
        y_ref[...] = _dot(hid.astype(BF16), wd_ref[0].astype(BF16))


def _experts(xs, tile_e, tile_src, tile_cnt, w_gate, w_up, w_down):
    n_tiles = tile_e.shape[0]
    wspec = lambda shape: pl.BlockSpec((1,) + shape, lambda n, te, ts, tc: (te[n], 0, 0))
    rows = pl.BlockSpec((EXPERT_TILE, D_MODEL), lambda n, te, ts, tc: (ts[n], 0))
    return pl.pallas_call(
        _moe_body,
        grid_spec=pltpu.PrefetchScalarGridSpec(
            num_scalar_prefetch=3,
            grid=(n_tiles,),
            in_specs=[rows, wspec((D_MODEL, D_EXPERT)), wspec((D_MODEL, D_EXPERT)),
                      wspec((D_EXPERT, D_MODEL))],
            out_specs=pl.BlockSpec((EXPERT_TILE, D_MODEL), lambda n, te, ts, tc: (n, 0))),
        out_shape=jax.ShapeDtypeStruct(xs.shape, F32),
        compiler_params=pltpu.CompilerParams(dimension_semantics=("arbitrary",),
                                             vmem_limit_bytes=VMEM_LIMIT),
        name="moe",
    )(tile_e, tile_src, tile_cnt, xs, w_gate, w_up, w_down)


def _combine_body(pos_ref, h_ref, wts_ref, ys_hbm, o_ref, y0buf, y1buf, sem):
    i = pl.program_id(0)
    tm = h_ref.shape[0]

    def fetch(r, c):
        a = 2 * (i * tm + r)
        pltpu.make_async_copy(ys_hbm.at[pl.ds(pos_ref[a], 1), :], y0buf.at[pl.ds(r, 1), :], sem).start()
        pltpu.make_async_copy(ys_hbm.at[pl.ds(pos_ref[a + 1], 1), :], y1buf.at[pl.ds(r, 1), :], sem).start()
        return c

    lax.fori_loop(0, tm, fetch, 0, unroll=8)
    pltpu.make_async_copy(ys_hbm.at[pl.ds(0, tm), :], y0buf, sem).wait()
    pltpu.make_async_copy(ys_hbm.at[pl.ds(0, tm), :], y1buf, sem).wait()
    wts = wts_ref[...]
    o_ref[...] = h_ref[...] + wts[:, 0:1] * y0buf[...] + wts[:, 1:2] * y1buf[...]


def _combine(h2d, wts, ys, pos_flat, tm):
    t = h2d.shape[0]
    return pl.pallas_call(
        _combine_body,
        grid_spec=pltpu.PrefetchScalarGridSpec(
            num_scalar_prefetch=1,
            grid=(t // tm,),
            in_specs=[pl.BlockSpec((tm, D_MODEL), lambda i, p: (i, 0)),
                      pl.BlockSpec((tm, LANES), lambda i, p: (i, 0)),
                      pl.BlockSpec(memory_space=pl.ANY)],
            out_specs=pl.BlockSpec((tm, D_MODEL), lambda i, p: (i, 0)),
            scratch_shapes=[pltpu.VMEM((tm, D_MODEL), F32), pltpu.VMEM((tm, D_MODEL), F32),
                            pltpu.SemaphoreType.DMA(())]),
        out_shape=jax.ShapeDtypeStruct((t, D_MODEL), F32),
        compiler_params=pltpu.CompilerParams(dimension_semantics=("arbitrary",),
                                             vmem_limit_bytes=VMEM_LIMIT),
        name="moe_combine",
    )(pos_flat, h2d, wts, ys)


def _moe(t2d, h2d, ids, wts, counts_f, w_gate, w_up, w_down, tm):
    t = t2d.shape[0]
    n_tiles = (2 * t) // EXPERT_TILE + N_EXPERTS
    counts = counts_f[0, :N_EXPERTS].astype(jnp.int32)
    tiles_e = (counts + EXPERT_TILE - 1) // EXPERT_TILE
    tile_end = jnp.cumsum(tiles_e)
    tile_start = tile_end - tiles_e
    total = tile_end[-1]
    tile_ids = jnp.arange(n_tiles, dtype=jnp.int32)
    tile_e = jnp.minimum(jnp.sum((tile_ids[:, None] >= tile_end[None, :]).astype(jnp.int32), axis=1),
                         N_EXPERTS - 1)
    onehot_t = (tile_e[:, None] == jnp.arange(N_EXPERTS)[None, :]).astype(jnp.int32)
    left = jnp.sum(onehot_t * (counts[None, :] - (tile_ids[:, None] - tile_start[None, :]) * EXPERT_TILE), axis=1)
    tile_cnt = jnp.where(tile_ids < total, jnp.clip(left, 0, EXPERT_TILE), 0).astype(jnp.int32)
    tile_src = jnp.minimum(tile_ids, total - 1).astype(jnp.int32)
    n_pad = jnp.stack([jnp.sum(jnp.where(tile_cnt > 0, EXPERT_TILE - tile_cnt, 0)),
                       jnp.sum((tile_cnt == 0).astype(jnp.int32))]).astype(jnp.int32)
    experts = ids[:, 0:2]
    onehot_a = (experts[:, :, None] == jnp.arange(N_EXPERTS)[None, None, :]).astype(jnp.int32)
    pos = jnp.sum(onehot_a * tile_start[None, None, :], axis=-1) * EXPERT_TILE + ids[:, 2:4]
    pos_flat = pos.reshape(-1).astype(jnp.int32)
    xs = _dispatch(t2d, pos_flat, tile_cnt, n_pad, n_tiles * EXPERT_TILE, tm)
    ys = _experts(xs, tile_e.astype(jnp.int32), tile_src, tile_cnt, w_gate, w_up, w_down)
    return _combine(h2d, wts, ys, pos_flat, tm)


def _rot_half_cols(wr):
    half = MLA_ROPE // 2
    return jnp.concatenate([-wr[..., half:], wr[..., :half]], axis=-1)


def _swap_halves(g):
    half = MLA_ROPE // 2
    return jnp.concatenate([g[..., half:], g[..., :half]], axis=-1)


def _group_sum_consts():
    sq = np.zeros((MLA_HEADS * LANES, LANES), np.float32)
    sqt = np.zeros((LANES, MLA_HEADS * LANES), np.float32)
    sk = np.zeros((MLA_HEADS * LANES, LANES), np.float32)
    skt = np.zeros((LANES, MLA_HEADS * LANES), np.float32)
    for h in range(MLA_HEADS):
        nope = slice(h * LANES, h * LANES + MLA_NOPE)
        rope = slice(h * LANES + MLA_NOPE, h * LANES + MLA_NOPE + MLA_ROPE)
        sq[nope, 2 * h] = 1.0 / MLA_NOPE
        sq[rope, 2 * h + 1] = 1.0 / MLA_ROPE
        sqt[2 * h, nope] = 1.0
        sqt[2 * h + 1, rope] = 1.0
        sk[nope, h] = 1.0 / MLA_NOPE
        skt[h, nope] = 1.0
    u2 = np.zeros((LANES, 2 * LANES), np.float32)
    u2[:, :LANES] = (np.arange(LANES)[:, None] > np.arange(LANES)[None, :]).astype(np.float32)
    u2[:, LANES:] = 1.0
    n3 = 3 * LANES
    u3 = np.ones((n3, n3 + LANES), np.float32)
    u3[:, :n3] = (np.arange(n3)[:, None] > np.arange(n3)[None, :]).astype(np.float32)
    shs = np.zeros((MLA_HEADS * 4, MLA_HEADS * 8), np.float32)
    for h in range(MLA_HEADS):
        shs[h * 4:(h + 1) * 4, h * 8:(h + 1) * 8] = 1.0 / MLA_NOPE
    as16 = lambda a: jnp.asarray(a, dtype=BF16)
    return dict(sq=as16(sq), sqt=as16(sqt), sk=as16(sk), skt=as16(skt), u2=as16(u2), u3=as16(u3), shs=as16(shs))


def _prep_weights(norm1_g, w_in, cq_norm_g, ckv_norm_g, w_uq, qn_norm_g, qr_norm_g, kr_norm_g,
                  w_uk, kn_norm_g, w_uv, sb_out_norm_g, mla_out_norm_g, w_o, norm2_g,
                  w_router_group, b_router_group, w_router_expert, b_router_expert):
    w = _group_sum_consts()
    wr = w_in[:, 1408:1440]
    pad96 = jnp.zeros((D_MODEL, LANES - MLA_ROPE), F32)
    w['win'] = jnp.concatenate([w_in[:, :512] * SB_SCALE, w_in[:, 512:1408], wr, pad96,
                                _rot_half_cols(wr), pad96], axis=1).astype(BF16)
    w['n1g'] = norm1_g[None, :]
    w['cqg'] = cq_norm_g[None, :]
    w['ckvg'] = ckv_norm_g[None, :]
    w['wa'] = jnp.pad(w_uq, ((0, 0), (0, 0), (0, LANES - MLA_NOPE - MLA_ROPE))).reshape(
        MLA_Q_LORA, MLA_HEADS * LANES).astype(BF16)
    w['wb'] = jnp.pad(_rot_half_cols(w_uq[:, :, MLA_NOPE:]),
                      ((0, 0), (0, 0), (MLA_NOPE, LANES - MLA_NOPE - MLA_ROPE))).reshape(
        MLA_Q_LORA, MLA_HEADS * LANES).astype(BF16)
    z32 = jnp.zeros((LANES - MLA_NOPE - MLA_ROPE,), F32)
    z64 = jnp.zeros((MLA_NOPE,), F32)
    z96 = jnp.zeros((LANES - MLA_ROPE,), F32)
    head_row = lambda v: jnp.tile(v, MLA_HEADS)[None, :]
    w['g1'] = head_row(jnp.concatenate([qn_norm_g * kn_norm_g * MLA_SCALE, jnp.zeros((64,), F32)]))
    w['g2'] = head_row(jnp.concatenate([z64, qr_norm_g * MLA_SCALE, z32]))
    w['g3'] = head_row(jnp.concatenate([z64, _swap_halves(qr_norm_g) * MLA_SCALE, z32]))
    w['gk1'] = jnp.concatenate([kr_norm_g, z96])[None, :]
    w['gk3'] = jnp.concatenate([_swap_halves(kr_norm_g), z96])[None, :]
    w['wukp'] = jnp.pad(w_uk, ((0, 0), (0, 0), (0, LANES - MLA_NOPE))).reshape(
        MLA_KV_LORA, MLA_HEADS * LANES).astype(BF16)
    w['wuv'] = w_uv.reshape(MLA_KV_LORA, MLA_WIDTH).astype(BF16)
    pairs = w_uv.reshape(MLA_KV_LORA, MLA_HEADS // 2, 2, MLA_V)
    zero_v = jnp.zeros_like(pairs[:, :, 0])
    keep_even = jnp.stack([pairs[:, :, 0], zero_v], axis=2)
    keep_odd = jnp.stack([zero_v, pairs[:, :, 1]], axis=2)
    w['wuv2'] = jnp.stack([keep_even, keep_odd], axis=2).reshape(MLA_KV_LORA, 2 * MLA_WIDTH).astype(BF16)
    w['wukt'] = w_uk.reshape(MLA_KV_LORA, MLA_HEADS * MLA_NOPE).T.astype(BF16)
    wabs_nope = jnp.pad(jnp.transpose(w_uk, (1, 2, 0)), ((0, 0), (0, LANES - MLA_NOPE), (0, 0)))
    e_r = np.zeros((MLA_HEADS, LANES, LANES), np.float32)
    for j in range(MLA_ROPE):
        e_r[:, MLA_NOPE + j, j] = 1.0
    w['wabs'] = jnp.concatenate([wabs_nope, jnp.asarray(e_r)], axis=-1).reshape(
        MLA_HEADS * LANES, 2 * LANES).astype(BF16)
    w['gsb'] = sb_out_norm_g[None, :]
    w['gmla'] = mla_out_norm_g[None, :]
    w['wo'] = w_o.astype(BF16)
    w['n2g'] = norm2_g[None, :]
    wr_all = jnp.pad(jnp.concatenate([w_router_group, w_router_expert], axis=1),
                     ((0, 0), (0, LANES - N_GROUPS - N_EXPERTS)))
    w['wrh'], w['wrl'] = _split2(wr_all)
    w['br'] = jnp.pad(jnp.concatenate([b_router_group, b_router_expert]),
                      (0, LANES - N_GROUPS - N_EXPERTS))[None, :]
    return w


def _rope_tables(pos):
    half = MLA_ROPE // 2
    inv_freq = ROPE_THETA ** (-jnp.arange(half, dtype=F32) / half)
    ang = pos.astype(F32)[:, None] * inv_freq[None, :]
    z = jnp.zeros((pos.shape[0], 32), F32)

    def slab(v):
        return jnp.concatenate([v, v, z, v, v, z], axis=1)

    return slab(jnp.cos(ang)), slab(jnp.sin(ang))


def _pad_rows(a, rows):
    return jnp.pad(a, ((0, rows - a.shape[0]), (0, 0)))


def _kv_out(a, batch, length, tail):
    return a.reshape((1, batch, length) + tail)


def kernel(x_prompt, x_sample, cache_sb_k, cache_sb_v, cache_mla_ckv, cache_mla_krope, page_table, meta_tokens, norm1_g, w_in, cq_norm_g, ckv_norm_g, w_uq, qn_norm_g, qr_norm_g, kr_norm_g, w_uk, kn_norm_g, w_uv, sb_out_norm_g, mla_out_norm_g, w_o, norm2_g, w_router_group, b_router_group, w_router_expert, b_router_expert, w_gate, w_up, w_down):
    batch, seq, _ = x_prompt.shape
    nb, nq, _ = x_sample.shape
    n_past = page_table.shape[1] * cache_sb_k.shape[2]
    assert nq == 4 and seq % MLA_TQ == 0 and page_table.shape[1] % CHUNK_PAGES == 0
    w = _prep_weights(norm1_g[0], w_in[0], cq_norm_g[0], ckv_norm_g[0], w_uq[0], qn_norm_g[0],
                      qr_norm_g[0], kr_norm_g[0], w_uk[0], kn_norm_g[0], w_uv[0], sb_out_norm_g[0],
                      mla_out_norm_g[0], w_o[0], norm2_g[0], w_router_group[0], b_router_group[0],
                      w_router_expert[0], b_router_expert[0])
    wg, wu, wd = w_gate[0], w_up[0], w_down[0]

    cos_m, sin_m = _rope_tables(jnp.arange(N_META))
    cos_p, sin_p = _rope_tables(N_META + jnp.arange(seq))
    cos_s, sin_s = _rope_tables(jnp.tile(n_past + jnp.arange(nq), nb))
    xp2 = x_prompt.reshape(batch * seq, D_MODEL)
    xs2 = x_sample.reshape(nb * nq, D_MODEL)
    tm_s = min(ROW_TILE, nb * nq)
    pm = _project(meta_tokens, cos_m, sin_m, w, N_META)
    pp = _project(xp2, cos_p, sin_p, w, ROW_TILE)
    ps = _project(xs2, cos_s, sin_s, w, tm_s)

    sbo_p = _sb_prompt(pp['qsb'], pp['ksb16'], pp['vsb16'], _pad_rows(pm['ksb16'], BLK),
                       _pad_rows(pm['vsb16'], BLK), w['u2'], batch, seq)
    mlao_p = _mla_prompt(pp['qmla'], pp['kmla'], pp['vmla2'], _pad_rows(pm['kmla'], BLK),
                         _pad_rows(pm['vmla2'], BLK), batch, seq)

    kT = jnp.transpose(cache_sb_k, (0, 1, 3, 4, 2))
    vT = jnp.transpose(cache_sb_v, (0, 1, 3, 4, 2))
    krT = jnp.transpose(cache_mla_krope, (0, 1, 3, 2))
    qsb_g = ps['qsb'].reshape(nb, nq, SB_KV_HEADS, 2, SB_HEAD_DIM).transpose(0, 2, 3, 1, 4).reshape(
        nb, SB_KV_HEADS, 8, SB_HEAD_DIM)
    pad_keys = lambda a: jnp.pad(a.reshape(nb, nq, SB_KV_HEADS, SB_HEAD_DIM).transpose(0, 2, 1, 3),
                                 ((0, 0), (0, 0), (0, 8 - nq), (0, 0)))
    ksbn_g = pad_keys(ps['ksb'])
    vsbn_g = pad_keys(ps['vsb'])
    q_rows = jnp.tile(ps['qmla'].reshape(nb, nq, MLA_HEADS * LANES), (1, MLA_HEADS, 1))
    slab_of_lane = jnp.arange(MLA_HEADS * LANES)[None, :] // LANES
    head_of_row = jnp.arange(MLA_HEADS * nq)[:, None] // nq
    qbd = jnp.where((slab_of_lane == head_of_row)[None], q_rows, jnp.zeros((), BF16))
    kmn = jnp.pad(ps['kmla'].reshape(nb, nq, -1), ((0, 0), (0, 16 - nq), (0, 0)))
    vmn = jnp.pad(ps['vmla'].reshape(nb, nq, -1), ((0, 0), (0, 16 - nq), (0, 0)))
    sbo_g, mlao_full = _sample_attention(page_table, kT, vT, cache_mla_ckv, krT, qsb_g, ksbn_g, vsbn_g,
                                         qbd, kmn, vmn, w)
    sbo_s = sbo_g.reshape(nb, SB_KV_HEADS, 2, nq, SB_HEAD_DIM).transpose(0, 3, 1, 2, 4).reshape(
        nb * nq, SB_WIDTH)
    mf = mlao_full.reshape(nb, MLA_HEADS, nq, MLA_HEADS, MLA_V)
    mlao_s = jnp.stack([mf[:, h, :, h, :] for h in range(MLA_HEADS)], axis=2).reshape(nb * nq, MLA_WIDTH)

    def channel_mix(sbo, mlao, x2d, tm):
        h, t, ids, wts, counts = _post_attention(sbo, mlao, x2d, w, tm)
        return _moe(t, h, ids, wts, counts, wg, wu, wd, tm)

    y_prompt = channel_mix(sbo_p, mlao_p, xp2, ROW_TILE).reshape(batch, seq, D_MODEL)
    y_sample = channel_mix(sbo_s, mlao_s, xs2, tm_s).reshape(nb, nq, D_MODEL)

    def with_meta(m, p, tail):
        width = int(np.prod(tail))
        mb = jnp.broadcast_to(m[None, :, :width], (batch, N_META, width))
        full = jnp.concatenate([mb, p[:, :width].reshape(batch, seq, width)], axis=1)
        return full.reshape((1, batch, seq + N_META) + tail)

    kv_tail = (SB_KV_HEADS, SB_HEAD_DIM)
    return (y_prompt, y_sample,
            with_meta(pm['ksb'], pp['ksb'], kv_tail), with_meta(pm['vsb'], pp['vsb'], kv_tail),
            with_meta(pm['ckv'], pp['ckv'], (MLA_KV_LORA,)), with_meta(pm['kr'], pp['kr'], (MLA_ROPE,)),
            ps['ksb'].reshape((1, nb, nq) + kv_tail), ps['vsb'].reshape((1, nb, nq) + kv_tail),
            ps['ckv'].reshape(1, nb, nq, MLA_KV_LORA), ps['kr'][:, :MLA_ROPE].reshape(1, nb, nq, MLA_ROPE))
```

```python
import functools

import numpy as np
import jax
import jax.numpy as jnp
from jax import lax
from jax.experimental import pallas as pl
from jax.experimental.pallas import tpu as pltpu

F32 = jnp.float32
BF16 = jnp.bfloat16

D_MODEL = 1024
N_META = 16
EPS = 1e-6
SB_HEADS = 8
SB_KV_HEADS = 4
SB_HEAD_DIM = 64
SB_WIDTH = SB_HEADS * SB_HEAD_DIM
SB_SCALE = SB_HEAD_DIM ** -0.5
MLA_HEADS = 8
MLA_Q_LORA = 256
MLA_KV_LORA = 128
MLA_NOPE = 64
MLA_ROPE = 32
MLA_V = 64
MLA_WIDTH = MLA_HEADS * MLA_V
MLA_SCALE = (MLA_NOPE + MLA_ROPE) ** -0.5
ROPE_THETA = 10000.0
N_GROUPS = 4
EXPERTS_PER_GROUP = 8
N_EXPERTS = N_GROUPS * EXPERTS_PER_GROUP
D_EXPERT = 256

LANES = 128
BLK = 128
MLA_BLK = 256
MLA_TQ = 512
PAGE = 128
CHUNK_PAGES = 32
SUB_PAGES = 8
ROW_TILE = 256
EXPERT_TILE = 256
VMEM_LIMIT = 56 * 1024 * 1024
SB_DEAD_LOG = -104.0
NEG_BIG = -1e30

_IN_COLS = 512 + 256 + 256 + 256 + 128 + 128 + 128


def _full(shape):
    nd = len(shape)
    return pl.BlockSpec(shape, lambda *_: (0,) * nd)


def _split2(x):
    hi = x.astype(BF16)
    lo = (x - hi.astype(F32)).astype(BF16)
    return hi, lo


def _dot(a, b):
    return jnp.dot(a, b, preferred_element_type=F32)


def _dot_nt(a, b):
    return lax.dot_general(a, b, (((1,), (1,)), ((), ())), preferred_element_type=F32)


def _softplus(z):
    return jnp.maximum(z, 0.0) + jnp.log1p(jnp.exp(-jnp.abs(z)))


def _proj_body(x_ref, cos_ref, sin_ref, n1g_ref, win_ref, cqg_ref, ckvg_ref, wa_ref, wb_ref,
               sq_ref, sqt_ref, g1_ref, g2_ref, g3_ref, gk1_ref, gk3_ref, wukp_ref, sk_ref,
               skt_ref, wuv_ref, wuv2_ref,
               qsb_ref, ksb_ref, vsb_ref, ksb16_ref, vsb16_ref, ckv_ref, kr_ref, qmla_ref,
               kmla_ref, vmla_ref, vmla2_ref):
    x = x_ref[...]
    xn = x * lax.rsqrt(jnp.mean(x * x, axis=-1, keepdims=True) + EPS) * n1g_ref[...]
    p = _dot(xn.astype(BF16), win_ref[...])
    qsb_ref[...] = p[:, 0:512]
    ksb = p[:, 512:768]
    vsb = p[:, 768:1024]
    ksb_ref[...] = ksb
    vsb_ref[...] = vsb
    ksb16_ref[...] = ksb.astype(BF16)
    vsb16_ref[...] = vsb.astype(BF16)

    cq = p[:, 1024:1280]
    cq = cq * lax.rsqrt(jnp.mean(cq * cq, axis=-1, keepdims=True) + EPS) * cqg_ref[...]
    ckv = p[:, 1280:1408]
    ckv = ckv * lax.rsqrt(jnp.mean(ckv * ckv, axis=-1, keepdims=True) + EPS) * ckvg_ref[...]
    ckv_ref[...] = ckv

    cos = cos_ref[...]
    sin = sin_ref[...]
    s1 = p[:, 1408:1536]
    s2 = p[:, 1536:1664]
    inv_kr = lax.rsqrt(jnp.sum(s1 * s1, axis=-1, keepdims=True) * (1.0 / MLA_ROPE) + EPS)
    kr = (s1 * (cos * gk1_ref[...]) + s2 * (sin * gk3_ref[...])) * inv_kr
    kr_ref[...] = kr

    cqb = cq.astype(BF16)
    a = _dot(cqb, wa_ref[...])
    b = _dot(cqb, wb_ref[...])
    hi, lo = _split2(a * a)
    msq = _dot(hi, sq_ref[...]) + _dot(lo, sq_ref[...])
    ihi, ilo = _split2(lax.rsqrt(msq + EPS))
    invf = _dot(ihi, sqt_ref[...]) + _dot(ilo, sqt_ref[...])

    ckvb = ckv.astype(BF16)
    kraw = _dot(ckvb, wukp_ref[...])
    khi, klo = _split2(kraw * kraw)
    kmsq = _dot(khi, sk_ref[...]) + _dot(klo, sk_ref[...])
    kihi, kilo = _split2(lax.rsqrt(kmsq + EPS))
    kinvf = _dot(kihi, skt_ref[...]) + _dot(kilo, skt_ref[...])
    kr_shift = pltpu.roll(kr, 64, axis=1)

    for h in range(MLA_HEADS):
        sl = slice(h * LANES, (h + 1) * LANES)
        qh = (a[:, sl] * (g1_ref[:, sl] + cos * g2_ref[:, sl])
              + b[:, sl] * (sin * g3_ref[:, sl])) * invf[:, sl]
        qmla_ref[:, sl] = qh.astype(BF16)
        kmla_ref[:, sl] = (kraw[:, sl] * kinvf[:, sl] + kr_shift).astype(BF16)
    vmla_ref[...] = _dot(ckvb, wuv_ref[...]).astype(BF16)
    vmla2_ref[...] = _dot(ckvb, wuv2_ref[...]).astype(BF16)


def _project(x2d, cos_t, sin_t, w, tm):
    t = x2d.shape[0]
    tbl_blocks = cos_t.shape[0] // tm
    row = lambda width: pl.BlockSpec((tm, width), lambda i: (i, 0))
    tbl = pl.BlockSpec((tm, LANES), lambda i: (i % tbl_blocks, 0))
    consts = [w['n1g'], w['win'], w['cqg'], w['ckvg'], w['wa'], w['wb'], w['sq'], w['sqt'],
              w['g1'], w['g2'], w['g3'], w['gk1'], w['gk3'], w['wukp'], w['sk'], w['skt'], w['wuv'], w['wuv2']]
    out_widths = [(512, F32), (256, F32), (256, F32), (256, BF16), (256, BF16), (128, F32),
                  (128, F32), (1024, BF16), (1024, BF16), (512, BF16), (1024, BF16)]
    outs = pl.pallas_call(
        _proj_body,
        grid=(t // tm,),
        in_specs=[row(D_MODEL), tbl, tbl] + [_full(c.shape) for c in consts],
        out_specs=[row(wd) for wd, _ in out_widths],
        out_shape=[jax.ShapeDtypeStruct((t, wd), dt) for wd, dt in out_widths],
        compiler_params=pltpu.CompilerParams(dimension_semantics=("arbitrary",),
                                             vmem_limit_bytes=VMEM_LIMIT),
        name="proj",
    )(x2d, cos_t, sin_t, *consts)
    names = ['qsb', 'ksb', 'vsb', 'ksb16', 'vsb16', 'ckv', 'kr', 'qmla', 'kmla', 'vmla', 'vmla2']
    return dict(zip(names, outs))


def _sb_weights(z, vis, u, carry):
    n = z.shape[1]
    sp = _softplus(z)
    log_rest = -sp
    if vis is not None:
        log_rest = jnp.where(vis, log_rest, 0.0)
    hi, lo = _split2(log_rest)
    cs = _dot(hi, u) + _dot(lo, u)
    wgt = jnp.exp(z - sp + cs[:, :n] + carry)
    if vis is not None:
        wgt = jnp.where(vis, wgt, 0.0)
    return wgt.astype(BF16), cs[:, n:]


def _sb_block(q2, kj, vj, vis, u2, carry_ref, acc_ref, k_is_transposed=False):
    z = _dot(q2, kj) if k_is_transposed else _dot_nt(q2, kj)
    c = carry_ref[...]
    wb, total = _sb_weights(z, vis, u2, c)
    acc_ref[...] += _dot_nt(wb, vj) if k_is_transposed else _dot(wb, vj)
    carry_ref[...] = c + total


def _sb_prompt_body(q_ref, k_ref, v_ref, km_ref, vm_ref, u2_ref, o_ref, q2_ref, carry_ref, acc_ref):
    i = pl.program_id(1)
    lane = lax.broadcasted_iota(jnp.int32, (1, LANES), 1)
    lane_lo = lane < 64
    n_rows = SB_HEADS * BLK
    half = n_rows // 2
    row_t = lax.broadcasted_iota(jnp.int32, (n_rows, 1), 0) % BLK
    vis_diag = lane < row_t
    vis_meta = jnp.broadcast_to(lane < N_META, (n_rows, LANES))
    u2 = u2_ref[...]
    for g in range(SB_KV_HEADS):
        slab = q_ref[:, g * LANES:(g + 1) * LANES]
        rolled = pltpu.roll(slab, 64, axis=1)
        if g % 2 == 0:
            q_r0 = jnp.where(lane_lo, slab, 0.0)
            q_r1 = jnp.where(lane_lo, rolled, 0.0)
        else:
            q_r0 = jnp.where(lane_lo, 0.0, rolled)
            q_r1 = jnp.where(lane_lo, 0.0, slab)
        q2_ref[(2 * g) * BLK:(2 * g + 1) * BLK, :] = q_r0.astype(BF16)
        q2_ref[(2 * g + 1) * BLK:(2 * g + 2) * BLK, :] = q_r1.astype(BF16)
    carry_ref[...] = jnp.zeros_like(carry_ref)
    acc_ref[...] = jnp.zeros_like(acc_ref)

    def block(k_of_pair, v_of_pair, vis):
        z = jnp.concatenate([_dot_nt(q2_ref[0:half, :], k_of_pair(0)),
                             _dot_nt(q2_ref[half:, :], k_of_pair(1))], axis=0)
        c = carry_ref[...]
        wb, total = _sb_weights(z, vis, u2, c)
        acc_ref[0:half, :] += _dot(wb[0:half], v_of_pair(0))
        acc_ref[half:, :] += _dot(wb[half:], v_of_pair(1))
        carry_ref[...] = c + total

    def run(j, vis):
        rows = pl.ds(pl.multiple_of(j * BLK, BLK), BLK)
        block(lambda p: k_ref[rows, p * LANES:(p + 1) * LANES],
              lambda p: v_ref[rows, p * LANES:(p + 1) * LANES], vis)

    run(i, vis_diag)

    def cond(st):
        j, alive = st
        return jnp.logical_and(j >= 0, alive > SB_DEAD_LOG)

    def body(st):
        j, _ = st
        run(j, None)
        return j - 1, jnp.max(carry_ref[...])

    _, alive = lax.while_loop(cond, body, (i - 1, jnp.max(carry_ref[...])))

    @pl.when(alive > SB_DEAD_LOG)
    def _():
        block(lambda p: km_ref[:, p * LANES:(p + 1) * LANES],
              lambda p: vm_ref[:, p * LANES:(p + 1) * LANES], vis_meta)

    for g in range(SB_KV_HEADS):
        o_r0 = acc_ref[(2 * g) * BLK:(2 * g + 1) * BLK, :]
        o_r1 = acc_ref[(2 * g + 1) * BLK:(2 * g + 2) * BLK, :]
        if g % 2 == 0:
            out = jnp.where(lane_lo, o_r0, pltpu.roll(o_r1, 64, axis=1))
        else:
            out = jnp.where(lane_lo, pltpu.roll(o_r0, 64, axis=1), o_r1)
        o_ref[:, g * LANES:(g + 1) * LANES] = out


def _sb_prompt(qsb, ksb16, vsb16, km16, vm16, u2, batch, seq):
    nq = seq // BLK
    kv_w = SB_KV_HEADS * SB_HEAD_DIM
    return pl.pallas_call(
        _sb_prompt_body,
        grid=(batch, nq),
        in_specs=[pl.BlockSpec((BLK, SB_WIDTH), lambda b, i: (b * nq + i, 0)),
                  pl.BlockSpec((seq, kv_w), lambda b, i: (b, 0)),
                  pl.BlockSpec((seq, kv_w), lambda b, i: (b, 0)),
                  _full(km16.shape), _full(vm16.shape), _full(u2.shape)],
        out_specs=pl.BlockSpec((BLK, SB_WIDTH), lambda b, i: (b * nq + i, 0)),
        out_shape=jax.ShapeDtypeStruct((batch * seq, SB_WIDTH), F32),
        scratch_shapes=[pltpu.VMEM((SB_HEADS * BLK, LANES), BF16), pltpu.VMEM((SB_HEADS * BLK, LANES), F32),
                        pltpu.VMEM((SB_HEADS * BLK, LANES), F32)],
        compiler_params=pltpu.CompilerParams(dimension_semantics=("arbitrary", "arbitrary"),
                                             vmem_limit_bytes=VMEM_LIMIT),
        name="sb_prompt",
    )(qsb, ksb16, vsb16, km16, vm16, u2)


def _mla_prompt_body(q_ref, k_ref, v2_ref, km_ref, vm2_ref, ones_ref, o_ref, m_ref, l_ref, acc_ref):
    i = pl.program_id(1)
    tq = MLA_TQ
    tk = MLA_BLK
    per_q = tq // tk
    lane = lax.broadcasted_iota(jnp.int32, (1, LANES), 1)
    lane_lo = lane < 64
    col = lax.broadcasted_iota(jnp.int32, (1, tk), 1)
    row_t = lax.broadcasted_iota(jnp.int32, (tq, 1), 0)
    vis_meta = jnp.broadcast_to(lane < N_META, (tq, LANES))
    m_ref[...] = jnp.full_like(m_ref, NEG_BIG)
    l_ref[...] = jnp.zeros_like(l_ref)
    acc_ref[...] = jnp.zeros_like(acc_ref)

    def block(load_k, load_v2, vis):
        for hp in range(MLA_HEADS // 2):
            ps, alphas = [], []
            for par in range(2):
                h = 2 * hp + par
                s = _dot_nt(q_ref[:, h * LANES:(h + 1) * LANES], load_k(h))
                if vis is not None:
                    s = jnp.where(vis, s, NEG_BIG)
                m_old = m_ref[h]
                m_new = jnp.maximum(m_old, jnp.max(s, axis=-1, keepdims=True))
                p = jnp.exp(s - jnp.tile(m_new, (1, s.shape[1] // LANES)))
                alpha = jnp.exp(m_old - m_new)
                m_ref[h] = m_new
                ps.append(p.astype(BF16))
                alphas.append(alpha)
            pcat = jnp.concatenate(ps, axis=1)
            vbd = jnp.concatenate([load_v2(2 * hp), load_v2(2 * hp + 1)], axis=0)
            scale = jnp.where(lane_lo, alphas[0], alphas[1])
            nk = pcat.shape[1] // 2
            ones = ones_ref[MLA_BLK - nk:MLA_BLK + nk, :]
            pv = _dot(pcat, jnp.concatenate([vbd, ones], axis=1))
            acc_ref[hp] = acc_ref[hp] * scale + pv[:, :LANES]
            l_ref[hp] = l_ref[hp] * scale + pv[:, LANES:]

    block(lambda h: km_ref[:, h * LANES:(h + 1) * LANES],
          lambda c: vm2_ref[:, c * LANES:(c + 1) * LANES], vis_meta)

    def key_block(j, vis):
        rows = pl.ds(pl.multiple_of(j * tk, tk), tk)
        block(lambda h: k_ref[rows, h * LANES:(h + 1) * LANES],
              lambda c: v2_ref[rows, c * LANES:(c + 1) * LANES], vis)

    def body(j, carry):
        key_block(j, None)
        return carry

    lax.fori_loop(0, i * per_q, body, 0)
    for d in range(per_q):
        key_block(i * per_q + d, col + d * tk <= row_t)
    for hp in range(MLA_HEADS // 2):
        o_ref[:, hp * LANES:(hp + 1) * LANES] = acc_ref[hp] / l_ref[hp]


def _mla_prompt(qmla, kmla, vmla2, kmm, vmm2, batch, seq):
    nq = seq // MLA_TQ
    width = MLA_HEADS * LANES
    ones_np = np.zeros((2 * MLA_BLK, LANES), np.float32)
    ones_np[:MLA_BLK, :64] = 1.0
    ones_np[MLA_BLK:, 64:] = 1.0
    ones_bd = jnp.asarray(ones_np, dtype=BF16)
    return pl.pallas_call(
        _mla_prompt_body,
        grid=(batch, nq),
        in_specs=[pl.BlockSpec((MLA_TQ, width), lambda b, i: (b * nq + i, 0)),
                  pl.BlockSpec((seq, width), lambda b, i: (b, 0)),
                  pl.BlockSpec((seq, width), lambda b, i: (b, 0)),
                  _full(kmm.shape), _full(vmm2.shape), _full(ones_bd.shape)],
        out_specs=pl.BlockSpec((MLA_TQ, MLA_WIDTH), lambda b, i: (b * nq + i, 0)),
        out_shape=jax.ShapeDtypeStruct((batch * seq, MLA_WIDTH), F32),
        scratch_shapes=[pltpu.VMEM((MLA_HEADS, MLA_TQ, LANES), F32),
                        pltpu.VMEM((MLA_HEADS // 2, MLA_TQ, LANES), F32),
                        pltpu.VMEM((MLA_HEADS // 2, MLA_TQ, LANES), F32)],
        compiler_params=pltpu.CompilerParams(dimension_semantics=("arbitrary", "arbitrary"),
                                             vmem_limit_bytes=VMEM_LIMIT),
        name="mla_prompt",
    )(qmla, kmla, vmla2, kmm, vmm2, ones_bd)


def _sample_body(pt_ref,
                 kT_hbm, vT_hbm, ckv_hbm, krT_hbm,
                 qsb_ref, ksbn_ref, vsbn_ref, qbd_ref, kmn_ref, vmn_ref,
                 wukt_ref, wabs_ref, shs_ref, wuv_ref, u2_ref, u3_ref,
                 sbo_ref, mlao_ref,
                 cbuf, rbuf, sbk, sbv, wq_ref, qr_ref, m_ref, l_ref, lat_ref,
                 carry_ref, sbacc_ref, sem_c, sem_r, sem_sb):
    b = pl.program_id(0)
    c = pl.program_id(1)
    nb = pl.num_programs(0)
    nc = pl.num_programs(1)
    n_pages = nc * CHUNK_PAGES
    step = b * nc + c
    slot = step % 2
    chunk = CHUNK_PAGES * PAGE
    rows_q = MLA_HEADS * 4

    def chunk_copies(bb, cc, sl):
        cps = []
        for pg in range(CHUNK_PAGES):
            page = pt_ref[bb, cc * CHUNK_PAGES + pg]
            cps.append(pltpu.make_async_copy(
                ckv_hbm.at[0, page], cbuf.at[sl, pl.ds(pg * PAGE, PAGE), :], sem_c.at[sl]))
            cps.append(pltpu.make_async_copy(
                krT_hbm.at[0, page], rbuf.at[sl, :, pl.ds(pg * PAGE, PAGE)], sem_r.at[sl]))
        return cps

    def sb_copies(bb, page_idx, sl):
        page = pt_ref[bb, page_idx]
        return [pltpu.make_async_copy(kT_hbm.at[0, page], sbk.at[sl], sem_sb.at[sl, 0]),
                pltpu.make_async_copy(vT_hbm.at[0, page], sbv.at[sl], sem_sb.at[sl, 1])]

    @pl.when(step == 0)
    def _():
        for cp in chunk_copies(0, 0, 0):
            cp.start()

    @pl.when(c == 0)
    def _():
        for sl in range(2):
            for cp in sb_copies(b, n_pages - 1 - sl, sl):
                cp.start()
        qa = _dot(qbd_ref[0], wabs_ref[...])
        wq_ref[0:512, :] = wukt_ref[...]
        wq_ref[512:512 + rows_q, :] = qa[:, :LANES].astype(BF16)
        qr_ref[...] = qa[:, LANES:LANES + MLA_ROPE].astype(BF16)
        m_ref[...] = jnp.full_like(m_ref, NEG_BIG)
        l_ref[...] = jnp.zeros_like(l_ref)
        lat_ref[...] = jnp.zeros_like(lat_ref)

    pltpu.make_async_copy(cbuf.at[slot], cbuf.at[slot], sem_c.at[slot]).wait()
    pltpu.make_async_copy(rbuf.at[slot], rbuf.at[slot], sem_r.at[slot]).wait()

    last = nb * nc - 1
    nxt = jnp.where(step < last, step + 1, 0)
    next_copies = chunk_copies(nxt // nc, nxt % nc, 1 - slot)
    n_sub = CHUNK_PAGES // SUB_PAGES
    per_sub = len(next_copies) // n_sub

    sub = SUB_PAGES * PAGE
    cbs, scores = [], []
    for u in range(n_sub):
        for cp in next_copies[u * per_sub:(u + 1) * per_sub]:
            cp.start()
        cols = slice(u * sub, (u + 1) * sub)
        cb = cbuf[slot, cols, :].astype(BF16)
        big = _dot_nt(wq_ref[...], cb)
        k2 = big[0:512] * big[0:512]
        part = k2.reshape(MLA_HEADS, MLA_NOPE // 8, 8, sub).sum(axis=1).reshape(MLA_HEADS * 8, sub)
        ksum = _dot(shs_ref[...], part.astype(BF16))
        scores.append(big[512:512 + rows_q] * lax.rsqrt(ksum + EPS)
                      + _dot(qr_ref[...], rbuf[slot, :, cols].astype(BF16)))
        cbs.append(cb)
    s = jnp.concatenate(scores, axis=1)
    m_old = m_ref[...]
    m_new = jnp.maximum(m_old, jnp.max(s, axis=-1, keepdims=True))
    p = jnp.exp(s - m_new).astype(BF16)
    alpha = jnp.exp(m_old - m_new)
    l_ref[...] = alpha * l_ref[...] + jnp.sum(p.astype(F32), axis=-1, keepdims=True)
    pv = _dot(p[:, 0:sub], cbs[0])
    for u in range(1, len(cbs)):
        pv += _dot(p[:, u * sub:(u + 1) * sub], cbs[u])
    lat_ref[...] = alpha * lat_ref[...] + pv
    m_ref[...] = m_new

    @pl.when(step == last)
    def _():
        pltpu.make_async_copy(cbuf.at[1 - slot], cbuf.at[1 - slot], sem_c.at[1 - slot]).wait()
        pltpu.make_async_copy(rbuf.at[1 - slot], rbuf.at[1 - slot], sem_r.at[1 - slot]).wait()

    @pl.when(c == nc - 1)
    def _():
        lane8 = lax.broadcasted_iota(jnp.int32, (rows_q, 16), 1)
        q_of_row = lax.broadcasted_iota(jnp.int32, (rows_q, 16), 0) % 4
        s_new = _dot_nt(qbd_ref[0], kmn_ref[0])
        s_new = jnp.where(lane8 <= q_of_row, s_new, NEG_BIG)
        m_old = m_ref[...]
        m_fin = jnp.maximum(m_old, jnp.max(s_new, axis=-1, keepdims=True))
        p_new = jnp.exp(s_new - m_fin)
        alpha = jnp.exp(m_old - m_fin)
        l_fin = alpha * l_ref[...] + jnp.sum(p_new, axis=-1, keepdims=True)
        o = _dot((alpha * lat_ref[...]).astype(BF16), wuv_ref[...]) + _dot(p_new.astype(BF16), vmn_ref[0])
        mlao_ref[0] = o / l_fin

        u2 = u2_ref[...]
        n_first = 3 * PAGE
        lane3 = lax.broadcasted_iota(jnp.int32, (1, n_first), 1)
        q_row = lax.broadcasted_iota(jnp.int32, (SB_KV_HEADS * 8, 1), 0) % 4
        vis_first = lane3 < 2 * PAGE + q_row
        zpad = jnp.zeros((PAGE - 8, SB_HEAD_DIM), F32)
        for sl in range(2):
            for cp in sb_copies(b, n_pages - 1 - sl, sl):
                cp.wait()
        vns, zs = [], []
        for g in range(SB_KV_HEADS):
            qg = qsb_ref[0, g].astype(BF16)
            kn = jnp.concatenate([ksbn_ref[0, g], zpad], axis=0).astype(BF16)
            vn = jnp.concatenate([vsbn_ref[0, g], zpad], axis=0).astype(BF16)
            zs.append(jnp.concatenate([_dot(qg, sbk[1, g].astype(BF16)), _dot(qg, sbk[0, g].astype(BF16)),
                                       _dot_nt(qg, kn)], axis=1))
            vns.append(vn)
        z = jnp.concatenate(zs, axis=0)
        wb, total = _sb_weights(z, vis_first, u3_ref[...], 0.0)
        carry_ref[...] = total
        for g in range(SB_KV_HEADS):
            wg = wb[g * 8:(g + 1) * 8]
            sbacc_ref[g * 8:(g + 1) * 8, :] = (
                _dot_nt(wg[:, 0:PAGE], sbv[1, g].astype(BF16))
                + _dot_nt(wg[:, PAGE:2 * PAGE], sbv[0, g].astype(BF16))
                + _dot(wg[:, 2 * PAGE:], vns[g]))

        def sb_page(sl):
            for g in range(SB_KV_HEADS):
                qg = qsb_ref[0, g].astype(BF16)
                rows = slice(g * 8, (g + 1) * 8)
                _sb_block(qg, sbk[sl, g].astype(BF16), sbv[sl, g].astype(BF16), None, u2,
                          carry_ref.at[rows], sbacc_ref.at[rows], k_is_transposed=True)

        def cond(st):
            j, alive = st
            return jnp.logical_and(j >= 0, alive > SB_DEAD_LOG)

        def body(st):
            j, _ = st
            cps = sb_copies(b, j, 0)
            for cp in cps:
                cp.start()
            for cp in cps:
                cp.wait()
            sb_page(0)
            return j - 1, jnp.max(carry_ref[...])

        lax.while_loop(cond, body, (n_pages - 3, jnp.max(carry_ref[...])))
        for g in range(SB_KV_HEADS):
            sbo_ref[0, g] = sbacc_ref[g * 8:(g + 1) * 8, :]


def _sample_attention(page_table, kT, vT, ckv_cache, krT, qsb_g, ksbn_g, vsbn_g, qbd, kmn, vmn, w):
    nb, n_pages = page_table.shape
    nc = n_pages // CHUNK_PAGES
    chunk = CHUNK_PAGES * PAGE
    rows_q = MLA_HEADS * 4
    per_b = lambda shape: pl.BlockSpec((1,) + shape, lambda b, c, pt: (b,) + (0,) * len(shape))
    const = lambda a: pl.BlockSpec(a.shape, lambda b, c, pt: (0,) * a.ndim)
    consts = [w['wukt'], w['wabs'], w['shs'], w['wuv'], w['u2'], w['u3']]
    any_spec = pl.BlockSpec(memory_space=pl.ANY)
    return pl.pallas_call(
        _sample_body,
        grid_spec=pltpu.PrefetchScalarGridSpec(
            num_scalar_prefetch=1,
            grid=(nb, nc),
            in_specs=[any_spec, any_spec, any_spec, any_spec,
                      per_b((SB_KV_HEADS, 8, SB_HEAD_DIM)), per_b((SB_KV_HEADS, 8, SB_HEAD_DIM)),
                      per_b((SB_KV_HEADS, 8, SB_HEAD_DIM)), per_b((rows_q, MLA_HEADS * LANES)),
                      per_b((16, MLA_HEADS * LANES)), per_b((16, MLA_WIDTH))]
                     + [const(a) for a in consts],
            out_specs=[per_b((SB_KV_HEADS, 8, SB_HEAD_DIM)), per_b((rows_q, MLA_WIDTH))],
            scratch_shapes=[
                pltpu.VMEM((2, chunk, MLA_KV_LORA), F32),
                pltpu.VMEM((2, MLA_ROPE, chunk), F32),
                pltpu.VMEM((2, SB_KV_HEADS, SB_HEAD_DIM, PAGE), F32),
                pltpu.VMEM((2, SB_KV_HEADS, SB_HEAD_DIM, PAGE), F32),
                pltpu.VMEM((512 + rows_q, MLA_KV_LORA), BF16),
                pltpu.VMEM((rows_q, MLA_ROPE), BF16),
                pltpu.VMEM((rows_q, 1), F32),
                pltpu.VMEM((rows_q, 1), F32),
                pltpu.VMEM((rows_q, MLA_KV_LORA), F32),
                pltpu.VMEM((SB_KV_HEADS * 8, LANES), F32),
                pltpu.VMEM((SB_KV_HEADS * 8, SB_HEAD_DIM), F32),
                pltpu.SemaphoreType.DMA((2,)),
                pltpu.SemaphoreType.DMA((2,)),
                pltpu.SemaphoreType.DMA((2, 2)),
            ]),
        out_shape=[jax.ShapeDtypeStruct((nb, SB_KV_HEADS, 8, SB_HEAD_DIM), F32),
                   jax.ShapeDtypeStruct((nb, rows_q, MLA_WIDTH), F32)],
        compiler_params=pltpu.CompilerParams(dimension_semantics=("arbitrary", "arbitrary"),
                                             vmem_limit_bytes=VMEM_LIMIT),
        name="sample_attn",
    )(page_table, kT, vT, ckv_cache, krT, qsb_g, ksbn_g, vsbn_g, qbd, kmn, vmn, *consts)


def _post_body(sbo_ref, mlao_ref, x_ref, gsb_ref, gmla_ref, wo_ref, n2g_ref, wrh_ref, wrl_ref,
               br_ref, ltri_ref, h_ref, t_ref, ids_ref, wts_ref, counts_ref, cnt_ref):
    sbo = sbo_ref[...]
    mlao = mlao_ref[...]
    m_sb = sbo * lax.rsqrt(jnp.mean(sbo * sbo, axis=-1, keepdims=True) + EPS) * gsb_ref[...]
    m_mla = mlao * lax.rsqrt(jnp.mean(mlao * mlao, axis=-1, keepdims=True) + EPS) * gmla_ref[...]
    h = (x_ref[...] + _dot(m_sb.astype(BF16), wo_ref[0:SB_WIDTH, :])
         + _dot(m_mla.astype(BF16), wo_ref[SB_WIDTH:, :]))
    h_ref[...] = h
    t = h * lax.rsqrt(jnp.mean(h * h, axis=-1, keepdims=True) + EPS) * n2g_ref[...]
    t_ref[...] = t
    thi, tlo = _split2(t)
    lg = (_dot(thi, wrh_ref[...]) + _dot(tlo, wrh_ref[...]) + _dot(thi, wrl_ref[...])
          + br_ref[...])
    tm = lg.shape[0]
    lane_i = lax.broadcasted_iota(jnp.int32, (tm, LANES), 1)
    lane = lane_i.astype(F32)
    big_i = jnp.float32(1 << 20)
    is_grp = lane < N_GROUPS
    mg = jnp.max(jnp.where(is_grp, lg, -jnp.inf), axis=-1, keepdims=True)
    gidx = jnp.min(jnp.where(is_grp & (lg == mg), lane, big_i), axis=-1, keepdims=True)
    p_grp = 1.0 / jnp.sum(jnp.where(is_grp, jnp.exp(lg - mg), 0.0), axis=-1, keepdims=True)
    lo_lane = N_GROUPS + gidx * EXPERTS_PER_GROUP
    sel = (lane >= lo_lane) & (lane < lo_lane + EXPERTS_PER_GROUP)
    v1 = jnp.max(jnp.where(sel, lg, -jnp.inf), axis=-1, keepdims=True)
    i1 = jnp.min(jnp.where(sel & (lg == v1), lane, big_i), axis=-1, keepdims=True)
    sel2 = sel & (lane != i1)
    v2 = jnp.max(jnp.where(sel2, lg, -jnp.inf), axis=-1, keepdims=True)
    i2 = jnp.min(jnp.where(sel2 & (lg == v2), lane, big_i), axis=-1, keepdims=True)
    e21 = jnp.exp(v2 - v1)
    w1 = p_grp / (1.0 + e21)
    w2 = w1 * e21
    e1 = i1 - N_GROUPS
    e2 = i2 - N_GROUPS
    @pl.when(pl.program_id(0) == 0)
    def _():
        cnt_ref[...] = jnp.zeros_like(cnt_ref)

    oh1 = jnp.where(lane == e1, 1.0, 0.0)
    oh2 = jnp.where(lane == e2, 1.0, 0.0)
    both = oh1 + oh2
    before = _dot(ltri_ref[...], both.astype(BF16)) + cnt_ref[0:1, :]
    r1 = jnp.sum(oh1 * before, axis=-1, keepdims=True)
    r2 = jnp.sum(oh2 * before, axis=-1, keepdims=True)
    cnt_new = cnt_ref[...] + jnp.sum(both, axis=0, keepdims=True)
    cnt_ref[...] = cnt_new
    counts_ref[...] = cnt_new
    ids = jnp.where(lane_i == 0, e1, jnp.where(lane_i == 1, e2,
                    jnp.where(lane_i == 2, r1, jnp.where(lane_i == 3, r2, 0.0))))
    ids_ref[...] = ids.astype(jnp.int32)
    wts_ref[...] = jnp.where(lane_i == 0, w1, jnp.where(lane_i == 1, w2, 0.0))


def _post_attention(sbo, mlao, x2d, w, tm):
    t = x2d.shape[0]
    row = lambda width: pl.BlockSpec((tm, width), lambda i: (i, 0))
    ltri = jnp.asarray(np.tril(np.ones((tm, tm), np.float32), -1), dtype=BF16)
    consts = [w['gsb'], w['gmla'], w['wo'], w['n2g'], w['wrh'], w['wrl'], w['br'], ltri]
    return pl.pallas_call(
        _post_body,
        grid=(t // tm,),
        in_specs=[row(SB_WIDTH), row(MLA_WIDTH), row(D_MODEL)] + [_full(c.shape) for c in consts],
        out_specs=[row(D_MODEL), row(D_MODEL), row(LANES), row(LANES), _full((8, LANES))],
        out_shape=[jax.ShapeDtypeStruct((t, D_MODEL), F32), jax.ShapeDtypeStruct((t, D_MODEL), F32),
                   jax.ShapeDtypeStruct((t, LANES), jnp.int32), jax.ShapeDtypeStruct((t, LANES), F32),
                   jax.ShapeDtypeStruct((8, LANES), F32)],
        scratch_shapes=[pltpu.VMEM((8, LANES), F32)],
        compiler_params=pltpu.CompilerParams(dimension_semantics=("arbitrary",),
                                             vmem_limit_bytes=VMEM_LIMIT),
        name="post_attn",
    )(sbo, mlao, x2d, *consts)


def _wait_rows(src, dst, sem, cnt):
    def wait8(_, c):
        pltpu.make_async_copy(src.at[pl.ds(0, 8), :], dst.at[pl.ds(0, 8), :], sem).wait()
        return c

    def wait1(_, c):
        pltpu.make_async_copy(src.at[pl.ds(0, 1), :], dst.at[pl.ds(0, 1), :], sem).wait()
        return c

    lax.fori_loop(0, cnt // 8, wait8, 0)
    lax.fori_loop(0, cnt % 8, wait1, 0)


def _dispatch_body(pos_ref, tile_cnt_ref, npad_ref, t_ref, xs_hbm, zbuf, sem, sem_pad, sem_tile):
    i = pl.program_id(0)
    tm = t_ref.shape[0]

    def empty_tile_copy(n):
        return pltpu.make_async_copy(zbuf, xs_hbm.at[pl.ds(n * EXPERT_TILE, EXPERT_TILE), :], sem_tile)

    @pl.when(i == 0)
    def _():
        zbuf[...] = jnp.zeros_like(zbuf)

        def per_tile(n, c):
            cnt = tile_cnt_ref[n]

            @pl.when(cnt == 0)
            def _():
                empty_tile_copy(n).start()

            def per_row(r, c2):
                pltpu.make_async_copy(zbuf.at[pl.ds(0, 1), :],
                                      xs_hbm.at[pl.ds(n * EXPERT_TILE + r, 1), :], sem_pad).start()
                return c2

            lax.fori_loop(jnp.where(cnt > 0, cnt, EXPERT_TILE), EXPERT_TILE, per_row, 0)
            return c

        lax.fori_loop(0, tile_cnt_ref.shape[0], per_tile, 0)

    def send(r, c):
        a = 2 * (i * tm + r)
        for k in range(2):
            pltpu.make_async_copy(t_ref.at[pl.ds(r, 1), :],
                                  xs_hbm.at[pl.ds(pos_ref[a + k], 1), :], sem).start()
        return c

    lax.fori_loop(0, tm, send, 0, unroll=8)
    for _ in range(2):
        pltpu.make_async_copy(t_ref, xs_hbm.at[pl.ds(0, tm), :], sem).wait()

    @pl.when(i == pl.num_programs(0) - 1)
    def _():
        _wait_rows(zbuf, xs_hbm, sem_pad, npad_ref[0])

        def wait_tile(_, c):
            empty_tile_copy(0).wait()
            return c

        lax.fori_loop(0, npad_ref[1], wait_tile, 0)


def _dispatch(t2d, pos_flat, tile_cnt, n_pad, n_rows, tm):
    t = t2d.shape[0]
    return pl.pallas_call(
        _dispatch_body,
        grid_spec=pltpu.PrefetchScalarGridSpec(
            num_scalar_prefetch=3,
            grid=(t // tm,),
            in_specs=[pl.BlockSpec((tm, D_MODEL), lambda i, *_: (i, 0))],
            out_specs=pl.BlockSpec(memory_space=pl.ANY),
            scratch_shapes=[pltpu.VMEM((EXPERT_TILE, D_MODEL), F32), pltpu.SemaphoreType.DMA(()),
                            pltpu.SemaphoreType.DMA(()), pltpu.SemaphoreType.DMA(())]),
        out_shape=jax.ShapeDtypeStruct((n_rows, D_MODEL), F32),
        compiler_params=pltpu.CompilerParams(dimension_semantics=("arbitrary",),
                                             vmem_limit_bytes=VMEM_LIMIT),
        name="moe_dispatch",
    )(pos_flat, tile_cnt, n_pad, t2d)


def _moe_body(tile_e_ref, tile_src_ref, tile_cnt_ref, x_ref, wg_ref, wu_ref, wd_ref, y_ref,
              wg16, wu16, wd16):
    n = pl.program_id(0)
    cnt = tile_cnt_ref[n]
    new_expert = jnp.logical_or(n == 0, tile_e_ref[n] != tile_e_ref[jnp.maximum(n - 1, 0)])

    @pl.when(new_expert)
    def _():
        wg16[...] = wg_ref[0].astype(BF16)
        wu16[...] = wu_ref[0].astype(BF16)
        wd16[...] = wd_ref[0].astype(BF16)

    @pl.when(cnt == 0)
    def _():
        y_ref[...] = jnp.zeros_like(y_ref)

    @pl.when(cnt > 0)
    def _():
        xb = x_ref[...].astype(BF16)
        gate = _dot(xb, wg16[...])
        up = _dot(xb, wu16[...])
        hid = gate * jax.nn.sigmoid(gate) * up
        y_ref[...] = _dot(hid.astype(BF16), wd16[...])


def _experts(xs, tile_e, tile_src, tile_cnt, w_gate, w_up, w_down):
    n_tiles = tile_e.shape[0]
    wspec = lambda shape: pl.BlockSpec((1,) + shape, lambda n, te, ts, tc: (te[n], 0, 0))
    rows = pl.BlockSpec((EXPERT_TILE, D_MODEL), lambda n, te, ts, tc: (ts[n], 0))
    return pl.pallas_call(
        _moe_body,
        grid_spec=pltpu.PrefetchScalarGridSpec(
            num_scalar_prefetch=3,
            grid=(n_tiles,),
            in_specs=[rows, wspec((D_MODEL, D_EXPERT)), wspec((D_MODEL, D_EXPERT)),
                      wspec((D_EXPERT, D_MODEL))],
            out_specs=pl.BlockSpec((EXPERT_TILE, D_MODEL), lambda n, te, ts, tc: (n, 0)),
            scratch_shapes=[pltpu.VMEM((D_MODEL, D_EXPERT), BF16), pltpu.VMEM((D_MODEL, D_EXPERT), BF16),
                            pltpu.VMEM((D_EXPERT, D_MODEL), BF16)]),
        out_shape=jax.ShapeDtypeStruct(xs.shape, F32),
        compiler_params=pltpu.CompilerParams(dimension_semantics=("arbitrary",),
                                             vmem_limit_bytes=VMEM_LIMIT),
        name="moe",
    )(tile_e, tile_src, tile_cnt, xs, w_gate, w_up, w_down)


def _combine_body(pos_ref, h_ref, wts_ref, ys_hbm, o_ref, y0buf, y1buf, sem):
    i = pl.program_id(0)
    tm = h_ref.shape[0]

    def fetch(r, c):
        a = 2 * (i * tm + r)
        pltpu.make_async_copy(ys_hbm.at[pl.ds(pos_ref[a], 1), :], y0buf.at[pl.ds(r, 1), :], sem).start()
        pltpu.make_async_copy(ys_hbm.at[pl.ds(pos_ref[a + 1], 1), :], y1buf.at[pl.ds(r, 1), :], sem).start()
        return c

    lax.fori_loop(0, tm, fetch, 0, unroll=8)
    pltpu.make_async_copy(ys_hbm.at[pl.ds(0, tm), :], y0buf, sem).wait()
    pltpu.make_async_copy(ys_hbm.at[pl.ds(0, tm), :], y1buf, sem).wait()
    wts = wts_ref[...]
    o_ref[...] = h_ref[...] + wts[:, 0:1] * y0buf[...] + wts[:, 1:2] * y1buf[...]


def _combine(h2d, wts, ys, pos_flat, tm):
    t = h2d.shape[0]
    return pl.pallas_call(
        _combine_body,
        grid_spec=pltpu.PrefetchScalarGridSpec(
            num_scalar_prefetch=1,
            grid=(t // tm,),
            in_specs=[pl.BlockSpec((tm, D_MODEL), lambda i, p: (i, 0)),
                      pl.BlockSpec((tm, LANES), lambda i, p: (i, 0)),
                      pl.BlockSpec(memory_space=pl.ANY)],
            out_specs=pl.BlockSpec((tm, D_MODEL), lambda i, p: (i, 0)),
            scratch_shapes=[pltpu.VMEM((tm, D_MODEL), F32), pltpu.VMEM((tm, D_MODEL), F32),
                            pltpu.SemaphoreType.DMA(())]),
        out_shape=jax.ShapeDtypeStruct((t, D_MODEL), F32),
        compiler_params=pltpu.CompilerParams(dimension_semantics=("arbitrary",),
                                             vmem_limit_bytes=VMEM_LIMIT),
        name="moe_combine",
    )(pos_flat, h2d, wts, ys)


def _moe(t2d, h2d, ids, wts, counts_f, w_gate, w_up, w_down, tm):
    t = t2d.shape[0]
    n_tiles = (2 * t) // EXPERT_TILE + N_EXPERTS
    counts = counts_f[0, :N_EXPERTS].astype(jnp.int32)
    tiles_e = (counts + EXPERT_TILE - 1) // EXPERT_TILE
    tile_end = jnp.cumsum(tiles_e)
    tile_start = tile_end - tiles_e
    total = tile_end[-1]
    tile_ids = jnp.arange(n_tiles, dtype=jnp.int32)
    tile_e = jnp.minimum(jnp.sum((tile_ids[:, None] >= tile_end[None, :]).astype(jnp.int32), axis=1),
                         N_EXPERTS - 1)
    onehot_t = (tile_e[:, None] == jnp.arange(N_EXPERTS)[None, :]).astype(jnp.int32)
    left = jnp.sum(onehot_t * (counts[None, :] - (tile_ids[:, None] - tile_start[None, :]) * EXPERT_TILE), axis=1)
    tile_cnt = jnp.where(tile_ids < total, jnp.clip(left, 0, EXPERT_TILE), 0).astype(jnp.int32)
    tile_src = jnp.minimum(tile_ids, total - 1).astype(jnp.int32)
    n_pad = jnp.stack([jnp.sum(jnp.where(tile_cnt > 0, EXPERT_TILE - tile_cnt, 0)),
                       jnp.sum((tile_cnt == 0).astype(jnp.int32))]).astype(jnp.int32)
    experts = ids[:, 0:2]
    onehot_a = (experts[:, :, None] == jnp.arange(N_EXPERTS)[None, None, :]).astype(jnp.int32)
    pos = jnp.sum(onehot_a * tile_start[None, None, :], axis=-1) * EXPERT_TILE + ids[:, 2:4]
    pos_flat = pos.reshape(-1).astype(jnp.int32)
    xs = _dispatch(t2d, pos_flat, tile_cnt, n_pad, n_tiles * EXPERT_TILE, tm)
    ys = _experts(xs, tile_e.astype(jnp.int32), tile_src, tile_cnt, w_gate, w_up, w_down)
    return _combine(h2d, wts, ys, pos_flat, tm)


def _rot_half_cols(wr):
    half = MLA_ROPE // 2
    return jnp.concatenate([-wr[..., half:], wr[..., :half]], axis=-1)


def _swap_halves(g):
    half = MLA_ROPE // 2
    return jnp.concatenate([g[..., half:], g[..., :half]], axis=-1)


def _group_sum_consts():
    sq = np.zeros((MLA_HEADS * LANES, LANES), np.float32)
    sqt = np.zeros((LANES, MLA_HEADS * LANES), np.float32)
    sk = np.zeros((MLA_HEADS * LANES, LANES), np.float32)
    skt = np.zeros((LANES, MLA_HEADS * LANES), np.float32)
    for h in range(MLA_HEADS):
        nope = slice(h * LANES, h * LANES + MLA_NOPE)
        rope = slice(h * LANES + MLA_NOPE, h * LANES + MLA_NOPE + MLA_ROPE)
        sq[nope, 2 * h] = 1.0 / MLA_NOPE
        sq[rope, 2 * h + 1] = 1.0 / MLA_ROPE
        sqt[2 * h, nope] = 1.0
        sqt[2 * h + 1, rope] = 1.0
        sk[nope, h] = 1.0 / MLA_NOPE
        skt[h, nope] = 1.0
    u2 = np.zeros((LANES, 2 * LANES), np.float32)
    u2[:, :LANES] = (np.arange(LANES)[:, None] > np.arange(LANES)[None, :]).astype(np.float32)
    u2[:, LANES:] = 1.0
    n3 = 3 * LANES
    u3 = np.ones((n3, n3 + LANES), np.float32)
    u3[:, :n3] = (np.arange(n3)[:, None] > np.arange(n3)[None, :]).astype(np.float32)
    shs = np.zeros((MLA_HEADS * 4, MLA_HEADS * 8), np.float32)
    for h in range(MLA_HEADS):
        shs[h * 4:(h + 1) * 4, h * 8:(h + 1) * 8] = 1.0 / MLA_NOPE
    as16 = lambda a: jnp.asarray(a, dtype=BF16)
    return dict(sq=as16(sq), sqt=as16(sqt), sk=as16(sk), skt=as16(skt), u2=as16(u2), u3=as16(u3), shs=as16(shs))


def _prep_weights(norm1_g, w_in, cq_norm_g, ckv_norm_g, w_uq, qn_norm_g, qr_norm_g, kr_norm_g,
                  w_uk, kn_norm_g, w_uv, sb_out_norm_g, mla_out_norm_g, w_o, norm2_g,
                  w_router_group, b_router_group, w_router_expert, b_router_expert):
    w = _group_sum_consts()
    wr = w_in[:, 1408:1440]
    pad96 = jnp.zeros((D_MODEL, LANES - MLA_ROPE), F32)
    w['win'] = jnp.concatenate([w_in[:, :512] * SB_SCALE, w_in[:, 512:1408], wr, pad96,
                                _rot_half_cols(wr), pad96], axis=1).astype(BF16)
    w['n1g'] = norm1_g[None, :]
    w['cqg'] = cq_norm_g[None, :]
    w['ckvg'] = ckv_norm_g[None, :]
    w['wa'] = jnp.pad(w_uq, ((0, 0), (0, 0), (0, LANES - MLA_NOPE - MLA_ROPE))).reshape(
        MLA_Q_LORA, MLA_HEADS * LANES).astype(BF16)
    w['wb'] = jnp.pad(_rot_half_cols(w_uq[:, :, MLA_NOPE:]),
                      ((0, 0), (0, 0), (MLA_NOPE, LANES - MLA_NOPE - MLA_ROPE))).reshape(
        MLA_Q_LORA, MLA_HEADS * LANES).astype(BF16)
    z32 = jnp.zeros((LANES - MLA_NOPE - MLA_ROPE,), F32)
    z64 = jnp.zeros((MLA_NOPE,), F32)
    z96 = jnp.zeros((LANES - MLA_ROPE,), F32)
    head_row = lambda v: jnp.tile(v, MLA_HEADS)[None, :]
    w['g1'] = head_row(jnp.concatenate([qn_norm_g * kn_norm_g * MLA_SCALE, jnp.zeros((64,), F32)]))
    w['g2'] = head_row(jnp.concatenate([z64, qr_norm_g * MLA_SCALE, z32]))
    w['g3'] = head_row(jnp.concatenate([z64, _swap_halves(qr_norm_g) * MLA_SCALE, z32]))
    w['gk1'] = jnp.concatenate([kr_norm_g, z96])[None, :]
    w['gk3'] = jnp.concatenate([_swap_halves(kr_norm_g), z96])[None, :]
    w['wukp'] = jnp.pad(w_uk, ((0, 0), (0, 0), (0, LANES - MLA_NOPE))).reshape(
        MLA_KV_LORA, MLA_HEADS * LANES).astype(BF16)
    w['wuv'] = w_uv.reshape(MLA_KV_LORA, MLA_WIDTH).astype(BF16)
    pairs = w_uv.reshape(MLA_KV_LORA, MLA_HEADS // 2, 2, MLA_V)
    zero_v = jnp.zeros_like(pairs[:, :, 0])
    keep_even = jnp.stack([pairs[:, :, 0], zero_v], axis=2)
    keep_odd = jnp.stack([zero_v, pairs[:, :, 1]], axis=2)
    w['wuv2'] = jnp.stack([keep_even, keep_odd], axis=2).reshape(MLA_KV_LORA, 2 * MLA_WIDTH).astype(BF16)
    w['wukt'] = w_uk.reshape(MLA_KV_LORA, MLA_HEADS * MLA_NOPE).T.astype(BF16)
    wabs_nope = jnp.pad(jnp.transpose(w_uk, (1, 2, 0)), ((0, 0), (0, LANES - MLA_NOPE), (0, 0)))
    e_r = np.zeros((MLA_HEADS, LANES, LANES), np.float32)
    for j in range(MLA_ROPE):
        e_r[:, MLA_NOPE + j, j] = 1.0
    w['wabs'] = jnp.concatenate([wabs_nope, jnp.asarray(e_r)], axis=-1).reshape(
        MLA_HEADS * LANES, 2 * LANES).astype(BF16)
    w['gsb'] = sb_out_norm_g[None, :]
    w['gmla'] = mla_out_norm_g[None, :]
    w['wo'] = w_o.astype(BF16)
    w['n2g'] = norm2_g[None, :]
    wr_all = jnp.pad(jnp.concatenate([w_router_group, w_router_expert], axis=1),
                     ((0, 0), (0, LANES - N_GROUPS - N_EXPERTS)))
    w['wrh'], w['wrl'] = _split2(wr_all)
    w['br'] = jnp.pad(jnp.concatenate([b_router_group, b_router_expert]),
                      (0, LANES - N_GROUPS - N_EXPERTS))[None, :]
    return w


def _rope_tables(pos):
    half = MLA_ROPE // 2
    inv_freq = ROPE_THETA ** (-jnp.arange(half, dtype=F32) / half)
    ang = pos.astype(F32)[:, None] * inv_freq[None, :]
    z = jnp.zeros((pos.shape[0], 32), F32)

    def slab(v):
        return jnp.concatenate([v, v, z, v, v, z], axis=1)

    return slab(jnp.cos(ang)), slab(jnp.sin(ang))


def _pad_rows(a, rows):
    return jnp.pad(a, ((0, rows - a.shape[0]), (0, 0)))


def _kv_out(a, batch, length, tail):
    return a.reshape((1, batch, length) + tail)


def kernel(x_prompt, x_sample, cache_sb_k, cache_sb_v, cache_mla_ckv, cache_mla_krope, page_table, meta_tokens, norm1_g, w_in, cq_norm_g, ckv_norm_g, w_uq, qn_norm_g, qr_norm_g, kr_norm_g, w_uk, kn_norm_g, w_uv, sb_out_norm_g, mla_out_norm_g, w_o, norm2_g, w_router_group, b_router_group, w_router_expert, b_router_expert, w_gate, w_up, w_down):
    batch, seq, _ = x_prompt.shape
    nb, nq, _ = x_sample.shape
    n_past = page_table.shape[1] * cache_sb_k.shape[2]
    assert nq == 4 and seq % MLA_TQ == 0 and page_table.shape[1] % CHUNK_PAGES == 0
    w = _prep_weights(norm1_g[0], w_in[0], cq_norm_g[0], ckv_norm_g[0], w_uq[0], qn_norm_g[0],
                      qr_norm_g[0], kr_norm_g[0], w_uk[0], kn_norm_g[0], w_uv[0], sb_out_norm_g[0],
                      mla_out_norm_g[0], w_o[0], norm2_g[0], w_router_group[0], b_router_group[0],
                      w_router_expert[0], b_router_expert[0])
    wg, wu, wd = w_gate[0], w_up[0], w_down[0]

    cos_m, sin_m = _rope_tables(jnp.arange(N_META))
    cos_p, sin_p = _rope_tables(N_META + jnp.arange(seq))
    cos_s, sin_s = _rope_tables(jnp.tile(n_past + jnp.arange(nq), nb))
    xp2 = x_prompt.reshape(batch * seq, D_MODEL)
    xs2 = x_sample.reshape(nb * nq, D_MODEL)
    tm_s = min(ROW_TILE, nb * nq)
    pm = _project(meta_tokens, cos_m, sin_m, w, N_META)
    pp = _project(xp2, cos_p, sin_p, w, ROW_TILE)
    ps = _project(xs2, cos_s, sin_s, w, tm_s)

    sbo_p = _sb_prompt(pp['qsb'], pp['ksb16'], pp['vsb16'], _pad_rows(pm['ksb16'], BLK),
                       _pad_rows(pm['vsb16'], BLK), w['u2'], batch, seq)
    mlao_p = _mla_prompt(pp['qmla'], pp['kmla'], pp['vmla2'], _pad_rows(pm['kmla'], BLK),
                         _pad_rows(pm['vmla2'], BLK), batch, seq)

    kT = jnp.transpose(cache_sb_k, (0, 1, 3, 4, 2))
    vT = jnp.transpose(cache_sb_v, (0, 1, 3, 4, 2))
    krT = jnp.transpose(cache_mla_krope, (0, 1, 3, 2))
    qsb_g = ps['qsb'].reshape(nb, nq, SB_KV_HEADS, 2, SB_HEAD_DIM).transpose(0, 2, 3, 1, 4).reshape(
        nb, SB_KV_HEADS, 8, SB_HEAD_DIM)
    pad_keys = lambda a: jnp.pad(a.reshape(nb, nq, SB_KV_HEADS, SB_HEAD_DIM).transpose(0, 2, 1, 3),
                                 ((0, 0), (0, 0), (0, 8 - nq), (0, 0)))
    ksbn_g = pad_keys(ps['ksb'])
    vsbn_g = pad_keys(ps['vsb'])
    q_rows = jnp.tile(ps['qmla'].reshape(nb, nq, MLA_HEADS * LANES), (1, MLA_HEADS, 1))
    slab_of_lane = jnp.arange(MLA_HEADS * LANES)[None, :] // LANES
    head_of_row = jnp.arange(MLA_HEADS * nq)[:, None] // nq
    qbd = jnp.where((slab_of_lane == head_of_row)[None], q_rows, jnp.zeros((), BF16))
    kmn = jnp.pad(ps['kmla'].reshape(nb, nq, -1), ((0, 0), (0, 16 - nq), (0, 0)))
    vmn = jnp.pad(ps['vmla'].reshape(nb, nq, -1), ((0, 0), (0, 16 - nq), (0, 0)))
    sbo_g, mlao_full = _sample_attention(page_table, kT, vT, cache_mla_ckv, krT, qsb_g, ksbn_g, vsbn_g,
                                         qbd, kmn, vmn, w)
    sbo_s = sbo_g.reshape(nb, SB_KV_HEADS, 2, nq, SB_HEAD_DIM).transpose(0, 3, 1, 2, 4).reshape(
        nb * nq, SB_WIDTH)
    mf = mlao_full.reshape(nb, MLA_HEADS, nq, MLA_HEADS, MLA_V)
    mlao_s = jnp.stack([mf[:, h, :, h, :] for h in range(MLA_HEADS)], axis=2).reshape(nb * nq, MLA_WIDTH)

    def channel_mix(sbo, mlao, x2d, tm):
        h, t, ids, wts, counts = _post_attention(sbo, mlao, x2d, w, tm)
        return _moe(t, h, ids, wts, counts, wg, wu, wd, tm)

    y_prompt = channel_mix(sbo_p, mlao_p, xp2, ROW_TILE).reshape(batch, seq, D_MODEL)
    y_sample = channel_mix(sbo_s, mlao_s, xs2, tm_s).reshape(nb, nq, D_MODEL)

    def with_meta(m, p, tail):
        width = int(np.prod(tail))
        mb = jnp.broadcast_to(m[None, :, :width], (batch, N_META, width))
        full = jnp.concatenate([mb, p[:, :width].reshape(batch, seq, width)], axis=1)
        return full.reshape((1, batch, seq + N_META) + tail)

    kv_tail = (SB_KV_HEADS, SB_HEAD_DIM)
    return (y_prompt, y_sample,
            with_meta(pm['ksb'], pp['ksb'], kv_tail), with_meta(pm['vsb'], pp['vsb'], kv_tail),
            with_meta(pm['ckv'], pp['ckv'], (MLA_KV_LORA,)), with_meta(pm['kr'], pp['kr'], (MLA_ROPE,)),
            ps['ksb'].reshape((1, nb, nq) + kv_tail), ps['vsb'].reshape((1, nb, nq) + kv_tail),
            ps['ckv'].reshape(1, nb, nq, MLA_KV_LORA), ps['kr'][:, :MLA_ROPE].reshape(1, nb, nq, MLA_ROPE))
```
